```python
import math
import jax, jax.numpy as jnp
from jax import lax
import numpy as np

D_MODEL = 2048
BATCH = 8
SEQ = 2048
DEPTH = 2

MIX_WIDTH = D_MODEL
NSA_HEADS = 8
NSA_HEAD_DIM = (MIX_WIDTH // 2) // NSA_HEADS
NSA_KV_HEADS = 2
NSA_GROUP = NSA_HEADS // NSA_KV_HEADS
CMP_LEN = 32
CMP_STRIDE = 16
CMP_HIDDEN = 512
SLC_LEN = 64
SLC_TOP = 16
N_FORCED_LOCAL = 2
FORCE_SCORE = 1.0e4
WINDOW = 512
MLA_HEADS = 8
MLA_Q_RANK = 512
MLA_KV_RANK = 256
MLA_NOPE_DIM = 128
MLA_ROPE_DIM = 64
MLA_V_DIM = (MIX_WIDTH // 2) // MLA_HEADS
ROPE_THETA = 10000.0
RMS_EPS = 1e-6
D_FF_DENSE = 5632
N_EXPERTS = 8
TOP_K = 2
D_FF_EXPERT = 7168
N_DENSE = (DEPTH + 1) // 2
N_MOE = DEPTH // 2
Q_BLOCK = 128
SLC_Q_BLOCK = 64

NSA_Q_COLS = NSA_HEADS * NSA_HEAD_DIM
NSA_KV_COLS = 6 * NSA_KV_HEADS * NSA_HEAD_DIM
NSA_GATE_COLS = 3 * NSA_HEADS
IN_SPLIT = (NSA_Q_COLS, NSA_KV_COLS, NSA_GATE_COLS, MLA_Q_RANK, MLA_KV_RANK, MLA_ROPE_DIM)
IN_COLS = NSA_Q_COLS + NSA_KV_COLS + NSA_GATE_COLS + MLA_Q_RANK + MLA_KV_RANK + MLA_ROPE_DIM

kernel_name = "hymba_nsa_mla_moe_trunk"


def rms_norm(x, g):
    xf = x.astype(jnp.float32)
    y = xf * lax.rsqrt(jnp.mean(xf * xf, axis=-1, keepdims=True) + RMS_EPS)
    return (y * g.astype(jnp.float32)).astype(x.dtype)


def rope_tables(seq, dim):
    inv = 1.0 / (ROPE_THETA ** (jnp.arange(0, dim, 2, dtype=jnp.float32) / dim))
    ang = jnp.arange(seq, dtype=jnp.float32)[:, None] * inv[None, :]
    return jnp.cos(ang), jnp.sin(ang)


def apply_rope(x, cos, sin):
    c = cos[:, None, :].astype(x.dtype)
    s = sin[:, None, :].astype(x.dtype)
    x1, x2 = jnp.split(x, 2, axis=-1)
    return jnp.concatenate([x1 * c - x2 * s, x1 * s + x2 * c], axis=-1)


def masked_softmax(scores, mask):
    s = jnp.where(mask, scores.astype(jnp.float32), -jnp.inf)
    m = jnp.max(s, axis=-1, keepdims=True)
    m = jnp.where(jnp.isfinite(m), m, 0.0)
    p = jnp.where(mask, jnp.exp(s - m), 0.0)
    return p / jnp.maximum(jnp.sum(p, axis=-1, keepdims=True), 1e-30)


def split_cols(proj):
    out = []
    off = 0
    for n in IN_SPLIT:
        out.append(proj[..., off:off + n])
        off += n
    return out


def compress_blocks(t, pos_emb, w1, w2):
    B, S, G, dh = t.shape
    n_cmp = (S - CMP_LEN) // CMP_STRIDE + 1
    idx = jnp.arange(n_cmp)[:, None] * CMP_STRIDE + jnp.arange(CMP_LEN)[None, :]
    blk = t[:, idx] + pos_emb[:, None, :].astype(t.dtype)
    blk = blk.transpose(0, 1, 3, 2, 4).reshape(B, n_cmp, G, CMP_LEN * dh)
    return jax.nn.gelu(blk @ w1) @ w2


def nsa_attention(q, kv, gate_logits, cos, sin, k_pos, k_w1, k_w2, v_pos, v_w1, v_w2):
    B, S = q.shape[:2]
    H, G, R, dh = NSA_HEADS, NSA_KV_HEADS, NSA_GROUP, NSA_HEAD_DIM
    q = apply_rope(q.reshape(B, S, H, dh), cos, sin) * (dh ** -0.5)
    qg = q.reshape(B, S, G, R, dh)
    kc, vc, ksl, vsl, kw, vw = [t.reshape(B, S, G, dh) for t in jnp.split(kv, 6, axis=-1)]
    kc = apply_rope(kc, cos, sin)
    ksl = apply_rope(ksl, cos, sin)
    kw = apply_rope(kw, cos, sin)
    pos = jnp.arange(S)

    k_cmp = compress_blocks(kc, k_pos, k_w1, k_w2)
    v_cmp = compress_blocks(vc, v_pos, v_w1, v_w2)
    n_cmp = k_cmp.shape[1]
    cmp_end = jnp.arange(n_cmp) * CMP_STRIDE + CMP_LEN - 1
    cmp_mask = cmp_end[None, :] <= pos[:, None]
    s_cmp = jnp.einsum('bsgrd,bcgd->bgrsc', qg, k_cmp)
    p_cmp = masked_softmax(s_cmp, cmp_mask)
    o_cmp = jnp.einsum('bgrsc,bcgd->bsgrd', p_cmp.astype(v_cmp.dtype), v_cmp)

    n_slc = S // SLC_LEN
    top_n = min(SLC_TOP, n_slc)
    cmp_tok = jnp.arange(n_cmp)[:, None] * CMP_STRIDE + jnp.arange(CMP_LEN)[None, :]
    overlap = jnp.mean(((cmp_tok[:, :, None] // SLC_LEN) == jnp.arange(n_slc)[None, None, :]).astype(jnp.float32), axis=1)
    imp = jnp.einsum('bgrsc,cj->bgsj', p_cmp, overlap)
    j = jnp.arange(n_slc)
    causal_blk = (j[None, :] * SLC_LEN) <= pos[:, None]
    dist = (pos // SLC_LEN)[:, None] - j[None, :]
    forced = (j[None, :] == 0) | ((dist >= 0) & (dist < N_FORCED_LOCAL))
    sel_score = jnp.where(causal_blk, jnp.where(forced, FORCE_SCORE, imp), -jnp.inf)
    sel = lax.top_k(sel_score, top_n)[1]
    tok = (sel[..., None] * SLC_LEN + jnp.arange(SLC_LEN)).reshape(B, G, S, top_n * SLC_LEN)

    ks_t = ksl.transpose(0, 2, 1, 3)
    vs_t = vsl.transpose(0, 2, 1, 3)
    q_t = qg.transpose(0, 2, 3, 1, 4)
    gather = jax.vmap(jax.vmap(lambda t, i: t[i]))

    def slc_block(c):
        st = c * SLC_Q_BLOCK
        qb = lax.dynamic_slice_in_dim(q_t, st, SLC_Q_BLOCK, axis=3)
        ib = lax.dynamic_slice_in_dim(tok, st, SLC_Q_BLOCK, axis=2)
        kb = gather(ks_t, ib)
        vb = gather(vs_t, ib)
        s = jnp.einsum('bgrqd,bgqkd->bgrqk', qb, kb)
        qpos = st + jnp.arange(SLC_Q_BLOCK)
        mask = ib[:, :, None] <= qpos[:, None]
        p = masked_softmax(s, mask)
        return jnp.einsum('bgrqk,bgqkd->bgrqd', p.astype(vb.dtype), vb)

    o_slc = lax.map(slc_block, jnp.arange(S // SLC_Q_BLOCK))
    o_slc = o_slc.transpose(1, 0, 4, 2, 3, 5).reshape(B, S, G, R, dh)

    nqb = S // Q_BLOCK
    band = WINDOW + Q_BLOCK
    key_pos = jnp.arange(nqb)[:, None] * Q_BLOCK - WINDOW + jnp.arange(band)[None, :]
    pad = ((0, 0), (WINDOW, 0), (0, 0), (0, 0))
    kb = jnp.pad(kw, pad)[:, key_pos + WINDOW]
    vb = jnp.pad(vw, pad)[:, key_pos + WINDOW]
    qb = qg.reshape(B, nqb, Q_BLOCK, G, R, dh)
    s_win = jnp.einsum('bnqgrd,bnkgd->bgrnqk', qb, kb)
    qpos = jnp.arange(nqb)[:, None] * Q_BLOCK + jnp.arange(Q_BLOCK)[None, :]
    diff = qpos[:, :, None] - key_pos[:, None, :]
    win_mask = (diff >= 0) & (diff < WINDOW) & (key_pos[:, None, :] >= 0)
    p_win = masked_softmax(s_win, win_mask)
    o_win = jnp.einsum('bgrnqk,bnkgd->bnqgrd', p_win.astype(vb.dtype), vb).reshape(B, S, G, R, dh)

    g = jax.nn.sigmoid(gate_logits.astype(jnp.float32)).astype(q.dtype).reshape(B, S, G, R, 3)
    o = g[..., 0:1] * o_cmp + g[..., 1:2] * o_slc + g[..., 2:3] * o_win
    return o.reshape(B, S, H * dh)


def mla_attention(c_q, c_kv, k_r, cos, sin, q_norm, w_uq, kv_norm, w_ukv):
    B, S = c_q.shape[:2]
    H = MLA_HEADS
    q = (rms_norm(c_q, q_norm) @ w_uq).reshape(B, S, H, MLA_NOPE_DIM + MLA_ROPE_DIM)
    q_nope = q[..., :MLA_NOPE_DIM]
    q_rope = apply_rope(q[..., MLA_NOPE_DIM:], cos, sin)
    kv = (rms_norm(c_kv, kv_norm) @ w_ukv).reshape(B, S, H, MLA_NOPE_DIM + MLA_V_DIM)
    k_nope = kv[..., :MLA_NOPE_DIM]
    v = kv[..., MLA_NOPE_DIM:]
    k_rope = apply_rope(k_r[:, :, None, :], cos, sin)[:, :, 0, :]
    scale = (MLA_NOPE_DIM + MLA_ROPE_DIM) ** -0.5
    qn = q_nope.transpose(0, 2, 1, 3)
    qr = q_rope.transpose(0, 2, 1, 3)
    kn = k_nope.transpose(0, 2, 1, 3)
    vt = v.transpose(0, 2, 1, 3)
    kpos = jnp.arange(S)

    def block(c):
        st = c * Q_BLOCK
        qnb = lax.dynamic_slice_in_dim(qn, st, Q_BLOCK, axis=2)
        qrb = lax.dynamic_slice_in_dim(qr, st, Q_BLOCK, axis=2)
        s = (jnp.einsum('bhqd,bhkd->bhqk', qnb, kn) + jnp.einsum('bhqd,bkd->bhqk', qrb, k_rope)) * scale
        mask = kpos[None, :] <= (st + jnp.arange(Q_BLOCK))[:, None]
        p = masked_softmax(s, mask)
        return jnp.einsum('bhqk,bhkd->bhqd', p.astype(vt.dtype), vt)

    o = lax.map(block, jnp.arange(S // Q_BLOCK))
    return o.transpose(1, 0, 3, 2, 4).reshape(B, S, H * MLA_V_DIM)


def swiglu(h, w1, w3, w2):
    return (jax.nn.silu(h @ w1) * (h @ w3)) @ w2


def moe_swiglu(h, router, w1, w3, w2):
    B, S, D = h.shape
    t = h.reshape(B * S, D)
    probs = jax.nn.softmax((t @ router).astype(jnp.float32), axis=-1)
    top_v, top_i = lax.top_k(probs, TOP_K)
    top_v = top_v / jnp.sum(top_v, axis=-1, keepdims=True)
    gates = jnp.sum(jax.nn.one_hot(top_i, N_EXPERTS, dtype=jnp.float32) * top_v[..., None], axis=1)
    out = jnp.zeros_like(t)
    for e in range(N_EXPERTS):
        y = swiglu(t, w1[e], w3[e], w2[e])
        out = out + gates[:, e:e + 1].astype(y.dtype) * y
    return out.reshape(B, S, D)


def setup_inputs(seed: int = 0) -> dict:
    key = jax.random.key(seed)
    ks = jax.random.split(key, 24)
    f32 = jnp.float32

    def nrm(k, shape, scale):
        return jax.random.normal(k, shape, f32) * scale

    def gain(k, shape):
        return 1.0 + 0.01 * jax.random.normal(k, shape, f32)

    L = DEPTH
    cmp_in = CMP_LEN * NSA_HEAD_DIM
    return {
        "x": nrm(ks[0], (BATCH, SEQ, D_MODEL), 1.0),
        "attn_norm": gain(ks[1], (L, D_MODEL)),
        "w_in": nrm(ks[2], (L, D_MODEL, IN_COLS), D_MODEL ** -0.5),
        "w_out": nrm(ks[3], (L, MIX_WIDTH, D_MODEL), MIX_WIDTH ** -0.5),
        "cmp_k_pos": nrm(ks[4], (L, CMP_LEN, NSA_HEAD_DIM), 0.02),
        "cmp_k_w1": nrm(ks[5], (L, cmp_in, CMP_HIDDEN), cmp_in ** -0.5),
        "cmp_k_w2": nrm(ks[6], (L, CMP_HIDDEN, NSA_HEAD_DIM), CMP_HIDDEN ** -0.5),
        "cmp_v_pos": nrm(ks[7], (L, CMP_LEN, NSA_HEAD_DIM), 0.02),
        "cmp_v_w1": nrm(ks[8], (L, cmp_in, CMP_HIDDEN), cmp_in ** -0.5),
        "cmp_v_w2": nrm(ks[9], (L, CMP_HIDDEN, NSA_HEAD_DIM), CMP_HIDDEN ** -0.5),
        "mla_q_norm": gain(ks[10], (L, MLA_Q_RANK)),
        "mla_w_uq": nrm(ks[11], (L, MLA_Q_RANK, MLA_HEADS * (MLA_NOPE_DIM + MLA_ROPE_DIM)), MLA_Q_RANK ** -0.5),
        "mla_kv_norm": gain(ks[12], (L, MLA_KV_RANK)),
        "mla_w_ukv": nrm(ks[13], (L, MLA_KV_RANK, MLA_HEADS * (MLA_NOPE_DIM + MLA_V_DIM)), MLA_KV_RANK ** -0.5),
        "ffn_norm": gain(ks[14], (L, D_MODEL)),
        "dense_w1": nrm(ks[15], (N_DENSE, D_MODEL, D_FF_DENSE), D_MODEL ** -0.5),
        "dense_w3": nrm(ks[16], (N_DENSE, D_MODEL, D_FF_DENSE), D_MODEL ** -0.5),
        "dense_w2": nrm(ks[17], (N_DENSE, D_FF_DENSE, D_MODEL), D_FF_DENSE ** -0.5),
        "router": nrm(ks[18], (N_MOE, D_MODEL, N_EXPERTS), D_MODEL ** -0.5),
        "moe_w1": nrm(ks[19], (N_MOE, N_EXPERTS, D_MODEL, D_FF_EXPERT), D_MODEL ** -0.5),
        "moe_w3": nrm(ks[20], (N_MOE, N_EXPERTS, D_MODEL, D_FF_EXPERT), D_MODEL ** -0.5),
        "moe_w2": nrm(ks[21], (N_MOE, N_EXPERTS, D_FF_EXPERT, D_MODEL), D_FF_EXPERT ** -0.5),
        "final_norm": gain(ks[22], (D_MODEL,)),
    }


def reference(x, attn_norm, w_in, w_out, cmp_k_pos, cmp_k_w1, cmp_k_w2, cmp_v_pos, cmp_v_w1, cmp_v_w2,
              mla_q_norm, mla_w_uq, mla_kv_norm, mla_w_ukv, ffn_norm, dense_w1, dense_w3, dense_w2,
              router, moe_w1, moe_w3, moe_w2, final_norm):
    S = x.shape[1]
    cos_n, sin_n = rope_tables(S, NSA_HEAD_DIM)
    cos_m, sin_m = rope_tables(S, MLA_ROPE_DIM)
    for i in range(DEPTH):
        h = rms_norm(x, attn_norm[i])
        nsa_q, nsa_kv, nsa_gate, c_q, c_kv, k_r = split_cols(h @ w_in[i])
        o_nsa = nsa_attention(nsa_q, nsa_kv, nsa_gate, cos_n, sin_n,
                              cmp_k_pos[i], cmp_k_w1[i], cmp_k_w2[i],
                              cmp_v_pos[i], cmp_v_w1[i], cmp_v_w2[i])
        o_mla = mla_attention(c_q, c_kv, k_r, cos_m, sin_m,
                              mla_q_norm[i], mla_w_uq[i], mla_kv_norm[i], mla_w_ukv[i])
        x = x + jnp.concatenate([o_nsa, o_mla], axis=-1) @ w_out[i]
        h = rms_norm(x, ffn_norm[i])
        j = i // 2
        if i % 2 == 0:
            x = x + swiglu(h, dense_w1[j], dense_w3[j], dense_w2[j])
        else:
            x = x + moe_swiglu(h, router[j], moe_w1[j], moe_w3[j], moe_w2[j])
    return rms_norm(x, final_norm)
```

```python
import functools

import jax
import jax.numpy as jnp
from jax import lax
from jax.experimental import pallas as pl
from jax.experimental.pallas import tpu as pltpu

F32 = jnp.float32
BF16 = jnp.bfloat16

NSA_HEADS = 8
NSA_KV_HEADS = 2
NSA_GROUP = NSA_HEADS // NSA_KV_HEADS
CMP_LEN = 32
CMP_STRIDE = 16
SLC_LEN = 64
SLC_SHIFT = 6
SLC_TOP = 16
N_FORCED_LOCAL = 2
FORCE_SCORE = 1.0e4
WINDOW = 512
MLA_HEADS = 8
MLA_NOPE_DIM = 128
MLA_ROPE_DIM = 64
MLA_V_DIM = 128
ROPE_THETA = 10000.0
RMS_EPS = 1e-6
TOP_K = 2

LANES = 128
NEG = -1.0e30
VMEM_LIMIT = 56 * 1024 * 1024


def _params(*sem):
    return pltpu.CompilerParams(dimension_semantics=sem, vmem_limit_bytes=VMEM_LIMIT)


def _rms(xf, g):
    return xf * lax.rsqrt(jnp.mean(xf * xf, axis=-1, keepdims=True) + RMS_EPS) * g


def _dot(a, b):
    return jnp.dot(a.astype(BF16), b.astype(BF16), preferred_element_type=F32)


def _dot_nt(a, b):
    return lax.dot_general(a.astype(BF16), b.astype(BF16), (((1,), (1,)), ((), ())),
                           preferred_element_type=F32)


def _mm_kernel(*refs, n_x, has_norm, has_tab, has_res):
    x_refs = refs[:n_x]
    w_refs = refs[n_x:2 * n_x]
    pos = 2 * n_x
    g_ref = tab_ref = res_ref = None
    if has_norm:
        g_ref = refs[pos]
        pos += 1
    if has_tab:
        tab_ref = refs[pos]
        pos += 1
    if has_res:
        res_ref = refs[pos]
        pos += 1
    o_ref = refs[pos]
    acc = None
    for x_ref, w_ref in zip(x_refs, w_refs):
        x = x_ref[...]
        if has_norm:
            x = _rms(x.astype(F32), g_ref[...])
        d = _dot(x, w_ref[...])
        acc = d if acc is None else acc + d
    if has_tab:
        acc = acc * tab_ref[...]
    if has_res:
        acc = acc + res_ref[...]
    o_ref[...] = acc.astype(o_ref.dtype)


def _mm(xs, ws, *, x_cols=None, gain=None, tab=None, res=None, tm, tn, out_dtype, name):
    n_x = len(xs)
    m = xs[0].shape[0]
    n = ws[0].shape[1]
    x_cols = x_cols or [0] * n_x
    in_specs, args = [], []
    for x, w, c in zip(xs, ws, x_cols):
        in_specs.append(pl.BlockSpec((tm, w.shape[0]), lambda i, j, c=c: (i, c)))
        args.append(x)
    for w in ws:
        in_specs.append(pl.BlockSpec((w.shape[0], tn), lambda i, j: (0, j)))
        args.append(w)
    if gain is not None:
        in_specs.append(pl.BlockSpec((1, gain.shape[-1]), lambda i, j: (0, 0)))
        args.append(gain.reshape(1, -1))
    if tab is not None:
        rb, cb = tab.shape[0] // tm, tab.shape[1] // tn
        in_specs.append(pl.BlockSpec((tm, tn), lambda i, j: (i % rb, j % cb)))
        args.append(tab)
    if res is not None:
        in_specs.append(pl.BlockSpec((tm, tn), lambda i, j: (i, j)))
        args.append(res)
    return pl.pallas_call(
        functools.partial(_mm_kernel, n_x=n_x, has_norm=gain is not None, has_tab=tab is not None,
                          has_res=res is not None),
        out_shape=jax.ShapeDtypeStruct((m, n), out_dtype),
        grid=(m // tm, n // tn),
        in_specs=in_specs,
        out_specs=pl.BlockSpec((tm, tn), lambda i, j: (i, j)),
        compiler_params=_params("parallel", "arbitrary"),
        name=name,
    )(*args)


def _nsa_prep_kernel(p_ref, cos_ref, sin_ref, q_ref, kc_ref, vc_ref, ks_ref, vs_ref, kw_ref, vw_ref):
    cos = cos_ref[...]
    sin = sin_ref[...]
    dh = LANES

    def rope(x):
        return x * cos + pltpu.roll(x, dh // 2, 1) * sin

    scale = dh ** -0.5
    for h in range(NSA_HEADS):
        q_ref[0, :, h * dh:(h + 1) * dh] = (rope(p_ref[0, :, h * dh:(h + 1) * dh]) * scale).astype(q_ref.dtype)
    base = NSA_HEADS * dh
    outs = (kc_ref, vc_ref, ks_ref, vs_ref, kw_ref, vw_ref)
    for t, o_ref in enumerate(outs):
        for g in range(NSA_KV_HEADS):
            c0 = base + (t * NSA_KV_HEADS + g) * dh
            v = p_ref[0, :, c0:c0 + dh]
            if t % 2 == 0:
                v = rope(v)
            o_ref[0, g] = v.astype(o_ref.dtype)


def _nsa_prep(proj, cos_t, sin_t, *, ts):
    b, s, _ = proj.shape
    g, dh = NSA_KV_HEADS, LANES
    width = (NSA_HEADS + 6 * g) * dh
    kv_spec = pl.BlockSpec((1, g, ts, dh), lambda bi, si: (bi, 0, si, 0))
    kv = lambda dt: jax.ShapeDtypeStruct((b, g, s, dh), dt)
    return pl.pallas_call(
        _nsa_prep_kernel,
        out_shape=(jax.ShapeDtypeStruct((b, s, NSA_HEADS * dh), BF16),
                   kv(F32), kv(F32), kv(BF16), kv(BF16), kv(BF16), kv(BF16)),
        grid=(b, s // ts),
        in_specs=[pl.BlockSpec((1, ts, width), lambda bi, si: (bi, si, 0)),
                  pl.BlockSpec((ts, dh), lambda bi, si: (si, 0)),
                  pl.BlockSpec((ts, dh), lambda bi, si: (si, 0))],
        out_specs=(pl.BlockSpec((1, ts, NSA_HEADS * dh), lambda bi, si: (bi, si, 0)),
                   kv_spec, kv_spec, kv_spec, kv_spec, kv_spec, kv_spec),
        compiler_params=_params("parallel", "parallel"),
        name="nsa_prep",
    )(proj, cos_t, sin_t)


def _compress_kernel(t_ref, pa_ref, pb_ref, w1a_ref, w1b_ref, w2_ref, o_ref):
    t = t_ref[0, 0]
    ya = _dot(t + pa_ref[...], w1a_ref[...])
    yb = _dot(t + pb_ref[...], w1b_ref[...])
    n = yb.shape[0]
    pre = ya + pltpu.roll(yb, n - 1, 0)
    o_ref[0, 0] = _dot(jax.nn.gelu(pre), w2_ref[...])


def _compress(t, pos_emb, w1, w2):
    b, g, s, dh = t.shape
    nb = s // CMP_STRIDE
    half = CMP_STRIDE * dh
    hid = w1.shape[1]
    t16 = t.reshape(b, g, nb, half)
    pa = pos_emb[:CMP_STRIDE].reshape(1, half)
    pb = pos_emb[CMP_STRIDE:].reshape(1, half)
    const = lambda shape: pl.BlockSpec(shape, lambda bi, gi: (0, 0))
    return pl.pallas_call(
        _compress_kernel,
        out_shape=jax.ShapeDtypeStruct((b, g, nb, dh), F32),
        grid=(b, g),
        in_specs=[pl.BlockSpec((1, 1, nb, half), lambda bi, gi: (bi, gi, 0, 0)),
                  const((1, half)), const((1, half)),
                  const((half, hid)), const((half, hid)), const((hid, dh))],
        out_specs=pl.BlockSpec((1, 1, nb, dh), lambda bi, gi: (bi, gi, 0, 0)),
        compiler_params=_params("parallel", "parallel"),
        name="nsa_compress",
    )(t16, pa, pb, w1[:half].astype(BF16), w1[half:].astype(BF16), w2.astype(BF16))


def _cmp_sel_kernel(q_ref, kc_ref, vc_ref, ov_ref, ocmp_ref, sel_ref, *, tq, n_cmp, n_slc, top_n):
    qi = pl.program_id(2)
    dh = LANES
    pos = qi * tq + lax.broadcasted_iota(jnp.int32, (tq, LANES), 0)
    lane = lax.broadcasted_iota(jnp.int32, (tq, LANES), 1)
    valid = (lane * CMP_STRIDE + (CMP_LEN - 1) <= pos) & (lane < n_cmp)
    kc = kc_ref[0, 0]
    vc = vc_ref[0, 0]
    psum = jnp.zeros((tq, LANES), F32)
    for r in range(NSA_GROUP):
        q = q_ref[0, :, r * dh:(r + 1) * dh]
        s = jnp.where(valid, _dot_nt(q, kc), NEG)
        m = jnp.max(s, axis=-1, keepdims=True)
        p = jnp.where(valid, jnp.exp(s - m), 0.0)
        p = p / jnp.maximum(jnp.sum(p, axis=-1, keepdims=True), 1e-30)
        ocmp_ref[0, :, r * dh:(r + 1) * dh] = _dot(p, vc)
        psum = psum + p
    hi = psum.astype(BF16)
    lo = psum - hi.astype(F32)
    imp = _dot(hi, ov_ref[...]) + _dot(lo, ov_ref[...])
    causal_blk = lane * SLC_LEN <= pos
    dist = lax.shift_right_logical(pos, SLC_SHIFT) - lane
    forced = (lane == 0) | ((dist >= 0) & (dist < N_FORCED_LOCAL))
    score = jnp.where(causal_blk, jnp.where(forced, FORCE_SCORE, imp), -jnp.inf)
    rank = jnp.zeros((tq, LANES), jnp.int32)
    for i in range(n_slc):
        col = score[:, i:i + 1]
        beats = (col > score) | ((col == score) & (lane > i))
        rank = rank + jnp.where(beats, 1, 0)
    sel = (rank < top_n) & (lane < n_slc)
    sel_ref[0, 0] = jnp.where(sel, 1.0, 0.0).astype(sel_ref.dtype)


def _cmp_sel(qr, k_cmp, v_cmp, overlap, *, tq, n_cmp, n_slc, top_n):
    b, s, _ = qr.shape
    g = NSA_KV_HEADS
    gw = NSA_GROUP * LANES
    nb = k_cmp.shape[2]
    return pl.pallas_call(
        functools.partial(_cmp_sel_kernel, tq=tq, n_cmp=n_cmp, n_slc=n_slc, top_n=top_n),
        out_shape=(jax.ShapeDtypeStruct((b, s, NSA_HEADS * LANES), F32),
                   jax.ShapeDtypeStruct((b, g, s, LANES), BF16)),
        grid=(b, g, s // tq),
        in_specs=[pl.BlockSpec((1, tq, gw), lambda bi, gi, qi: (bi, qi, gi)),
                  pl.BlockSpec((1, 1, nb, LANES), lambda bi, gi, qi: (bi, gi, 0, 0)),
                  pl.BlockSpec((1, 1, nb, LANES), lambda bi, gi, qi: (bi, gi, 0, 0)),
                  pl.BlockSpec((LANES, LANES), lambda bi, gi, qi: (0, 0))],
        out_specs=(pl.BlockSpec((1, tq, gw), lambda bi, gi, qi: (bi, qi, gi)),
                   pl.BlockSpec((1, 1, tq, LANES), lambda bi, gi, qi: (bi, gi, qi, 0))),
        compiler_params=_params("parallel", "parallel", "parallel"),
        name="nsa_cmp_select",
    )(qr, k_cmp, v_cmp, overlap)


def _flash_init(m_sc, l_sc, acc_sc):
    m_sc[...] = jnp.full(m_sc.shape, NEG, F32)
    l_sc[...] = jnp.zeros(l_sc.shape, F32)
    acc_sc[...] = jnp.zeros(acc_sc.shape, F32)


def _flash_step(q, k, v, mask, m_sc, l_sc, acc_sc, h):
    s = jnp.where(mask, _dot_nt(q, k), NEG)
    m_prev = m_sc[h]
    m_new = jnp.maximum(m_prev, jnp.max(s, axis=-1, keepdims=True))
    alpha = jnp.exp(m_prev - m_new)
    p = jnp.where(mask, jnp.exp(s - m_new), 0.0)
    l_sc[h] = alpha * l_sc[h] + jnp.sum(p, axis=-1, keepdims=True)
    acc_sc[h] = alpha * acc_sc[h] + _dot(p, v)
    m_sc[h] = m_new


def _flash_out(l_sc, acc_sc, h):
    return acc_sc[h] / jnp.maximum(l_sc[h], 1e-30)


def _nsa_attn_kernel(q_ref, ks_ref, vs_ref, kw_ref, vw_ref, sel_ref, et_ref, ocmp_ref, gate_ref, o_ref,
                     m_sc, l_sc, acc_sc, comb_sc, *, tq, tk):
    gi = pl.program_id(1)
    qi = pl.program_id(2)
    dh = LANES
    q0 = qi * tq
    qpos = q0 + lax.broadcasted_iota(jnp.int32, (tq, tk), 0)
    krel = lax.broadcasted_iota(jnp.int32, (tq, tk), 1)
    gates = jax.nn.sigmoid(gate_ref[0])
    n_gate = 3 * NSA_GROUP

    def gate_col(r, branch):
        col = None
        for g in range(NSA_KV_HEADS):
            c = g * n_gate + r * 3 + branch
            v = gates[:, c:c + 1]
            col = v if col is None else jnp.where(gi == g, v, col)
        return col

    _flash_init(m_sc, l_sc, acc_sc)
    sel = sel_ref[0, 0]

    def slc_body(j, carry):
        k0 = pl.multiple_of(j * tk, tk)
        chosen = _dot_nt(sel, et_ref[pl.ds(k0, tk), :])
        mask = (chosen > 0.5) & (k0 + krel <= qpos)
        k = ks_ref[0, 0, pl.ds(k0, tk), :]
        v = vs_ref[0, 0, pl.ds(k0, tk), :]
        for r in range(NSA_GROUP):
            _flash_step(q_ref[0, :, r * dh:(r + 1) * dh], k, v, mask, m_sc, l_sc, acc_sc, r)
        return carry

    lax.fori_loop(0, (q0 + tq + tk - 1) // tk, slc_body, 0)
    for r in range(NSA_GROUP):
        comb_sc[:, r * dh:(r + 1) * dh] = (gate_col(r, 0) * ocmp_ref[0, :, r * dh:(r + 1) * dh]
                                            + gate_col(r, 1) * _flash_out(l_sc, acc_sc, r))

    _flash_init(m_sc, l_sc, acc_sc)

    def win_body(j, carry):
        k0 = pl.multiple_of(j * tk, tk)
        diff = qpos - (k0 + krel)
        mask = (diff >= 0) & (diff < WINDOW)
        k = kw_ref[0, 0, pl.ds(k0, tk), :]
        v = vw_ref[0, 0, pl.ds(k0, tk), :]
        for r in range(NSA_GROUP):
            _flash_step(q_ref[0, :, r * dh:(r + 1) * dh], k, v, mask, m_sc, l_sc, acc_sc, r)
        return carry

    lax.fori_loop(jnp.maximum(q0 - (WINDOW - 1), 0) // tk, (q0 + tq + tk - 1) // tk, win_body, 0)
    for r in range(NSA_GROUP):
        o_ref[0, :, r * dh:(r + 1) * dh] = (comb_sc[:, r * dh:(r + 1) * dh]
                                            + gate_col(r, 2) * _flash_out(l_sc, acc_sc, r)).astype(o_ref.dtype)


def _nsa_attn(qr, ks, vs, kw, vw, sel, e_t, o_cmp, proj, gate_col_block, *, tq, tk):
    b, s, _ = qr.shape
    g = NSA_KV_HEADS
    gw = NSA_GROUP * LANES
    kv_spec = pl.BlockSpec((1, 1, s, LANES), lambda bi, gi, qi: (bi, gi, 0, 0))
    q_spec = pl.BlockSpec((1, tq, gw), lambda bi, gi, qi: (bi, qi, gi))
    return pl.pallas_call(
        functools.partial(_nsa_attn_kernel, tq=tq, tk=tk),
        out_shape=jax.ShapeDtypeStruct((b, s, NSA_HEADS * LANES), BF16),
        grid=(b, g, s // tq),
        in_specs=[q_spec, kv_spec, kv_spec, kv_spec, kv_spec,
                  pl.BlockSpec((1, 1, tq, LANES), lambda bi, gi, qi: (bi, gi, qi, 0)),
                  pl.BlockSpec((s, LANES), lambda bi, gi, qi: (0, 0)),
                  q_spec,
                  pl.BlockSpec((1, tq, LANES), lambda bi, gi, qi: (bi, qi, gate_col_block))],
        out_specs=q_spec,
        scratch_shapes=[pltpu.VMEM((NSA_GROUP, tq, 1), F32), pltpu.VMEM((NSA_GROUP, tq, 1), F32),
                        pltpu.VMEM((NSA_GROUP, tq, LANES), F32), pltpu.VMEM((tq, gw), F32)],
        compiler_params=_params("parallel", "parallel", "parallel"),
        name="nsa_slc_win",
    )(qr, ks, vs, kw, vw, sel, e_t, o_cmp, proj)


def _mla_attn_kernel(q_ref, kn_ref, v_ref, kr_ref, tk_ref, o_ref, kcat_sc, m_sc, l_sc, acc_sc, *, tq, tk):
    qi = pl.program_id(2)
    nope = MLA_NOPE_DIM

    @pl.when(qi == 0)
    def _():
        prod = kr_ref[0] * tk_ref[...]
        rope2 = prod + pltpu.roll(prod, MLA_ROPE_DIM, 1)
        kcat_sc[:, :nope] = kn_ref[0]
        kcat_sc[:, nope:] = rope2.astype(kcat_sc.dtype)

    q0 = qi * tq
    qpos = q0 + lax.broadcasted_iota(jnp.int32, (tq, tk), 0)
    krel = lax.broadcasted_iota(jnp.int32, (tq, tk), 1)
    _flash_init(m_sc, l_sc, acc_sc)
    q = q_ref[0]

    def body(j, carry):
        k0 = pl.multiple_of(j * tk, tk)
        mask = k0 + krel <= qpos
        _flash_step(q, kcat_sc[pl.ds(k0, tk), :], v_ref[0, pl.ds(k0, tk), :], mask, m_sc, l_sc, acc_sc, 0)
        return carry

    lax.fori_loop(0, (q0 + tq + tk - 1) // tk, body, 0)
    o_ref[0] = _flash_out(l_sc, acc_sc, 0).astype(o_ref.dtype)


def _mla_attn(q2, kv_up, proj, kr_col_block, tab_k, *, tq, tk):
    b, s, _ = q2.shape
    hq = MLA_NOPE_DIM + 2 * MLA_ROPE_DIM
    return pl.pallas_call(
        functools.partial(_mla_attn_kernel, tq=tq, tk=tk),
        out_shape=jax.ShapeDtypeStruct((b, s, MLA_HEADS * MLA_V_DIM), BF16),
        grid=(b, MLA_HEADS, s // tq),
        in_specs=[pl.BlockSpec((1, tq, hq), lambda bi, hi, qi: (bi, qi, hi)),
                  pl.BlockSpec((1, s, MLA_NOPE_DIM), lambda bi, hi, qi: (bi, 0, 2 * hi)),
                  pl.BlockSpec((1, s, MLA_V_DIM), lambda bi, hi, qi: (bi, 0, 2 * hi + 1)),
                  pl.BlockSpec((1, s, LANES), lambda bi, hi, qi: (bi, 0, kr_col_block)),
                  pl.BlockSpec((s, LANES), lambda bi, hi, qi: (0, 0))],
        out_specs=pl.BlockSpec((1, tq, MLA_V_DIM), lambda bi, hi, qi: (bi, qi, hi)),
        scratch_shapes=[pltpu.VMEM((s, hq), BF16), pltpu.VMEM((1, tq, 1), F32), pltpu.VMEM((1, tq, 1), F32),
                        pltpu.VMEM((1, tq, MLA_V_DIM), F32)],
        compiler_params=_params("parallel", "parallel", "arbitrary"),
        name="mla_attn",
    )(q2, kv_up, kv_up, proj, tab_k)


def _ffn_kernel(*refs, gated):
    if gated:
        x_ref, g_ref, w1_ref, w3_ref, w2_ref, gate_ref, o_ref, hn_sc, acc_sc = refs
    else:
        x_ref, g_ref, w1_ref, w3_ref, w2_ref, o_ref, hn_sc, acc_sc = refs
    e = pl.program_id(1)
    f = pl.program_id(2)

    @pl.when((e == 0) & (f == 0))
    def _():
        hn_sc[...] = _rms(x_ref[...], g_ref[...]).astype(hn_sc.dtype)
        acc_sc[...] = jnp.zeros(acc_sc.shape, F32)

    hn = hn_sc[...]
    z = jax.nn.silu(_dot(hn, w1_ref[0])) * _dot(hn, w3_ref[0])
    y = _dot(z, w2_ref[0])
    if gated:
        y = gate_ref[0] * y
    acc_sc[...] += y

    @pl.when((e == pl.num_programs(1) - 1) & (f == pl.num_programs(2) - 1))
    def _():
        o_ref[...] = x_ref[...] + acc_sc[...]


def _ffn(x, gain, w1, w3, w2, gates=None, *, tm, tf, name):
    t, d = x.shape
    ne, _, ff = w1.shape
    in_specs = [pl.BlockSpec((tm, d), lambda i, e, f: (i, 0)),
                pl.BlockSpec((1, d), lambda i, e, f: (0, 0)),
                pl.BlockSpec((1, d, tf), lambda i, e, f: (e, 0, f)),
                pl.BlockSpec((1, d, tf), lambda i, e, f: (e, 0, f)),
                pl.BlockSpec((1, tf, d), lambda i, e, f: (e, f, 0))]
    args = [x, gain.reshape(1, d), w1, w3, w2]
    if gates is not None:
        in_specs.append(pl.BlockSpec((1, tm, 1), lambda i, e, f: (e, i, 0)))
        args.append(gates)
    return pl.pallas_call(
        functools.partial(_ffn_kernel, gated=gates is not None),
        out_shape=jax.ShapeDtypeStruct((t, d), F32),
        grid=(t // tm, ne, ff // tf),
        in_specs=in_specs,
        out_specs=pl.BlockSpec((tm, d), lambda i, e, f: (i, 0)),
        scratch_shapes=[pltpu.VMEM((tm, d), BF16), pltpu.VMEM((tm, d), F32)],
        compiler_params=_params("parallel", "arbitrary", "arbitrary"),
        name=name,
    )(*args)


def _router_kernel(x_ref, g_ref, w_ref, o_ref, *, n_exp):
    hn = _rms(x_ref[...], g_ref[...])
    w = w_ref[...]
    h_hi = hn.astype(BF16)
    h_lo = hn - h_hi.astype(F32)
    w_hi = w.astype(BF16)
    w_lo = w - w_hi.astype(F32)
    logits = _dot(h_hi, w_hi) + _dot(h_hi, w_lo) + _dot(h_lo, w_hi)
    lane = lax.broadcasted_iota(jnp.int32, logits.shape, 1)
    valid = lane < n_exp
    lg = jnp.where(valid, logits, NEG)
    ex = jnp.where(valid, jnp.exp(lg - jnp.max(lg, axis=-1, keepdims=True)), 0.0)
    probs = ex / jnp.sum(ex, axis=-1, keepdims=True)
    gates = jnp.zeros(logits.shape, F32)
    rest = jnp.where(valid, probs, -1.0)
    tops = []
    for _ in range(TOP_K):
        v = jnp.max(rest, axis=-1, keepdims=True)
        idx = jnp.min(jnp.where(rest == v, lane, LANES), axis=-1, keepdims=True)
        tops.append((v, idx))
        rest = jnp.where(lane == idx, -1.0, rest)
    den = tops[0][0]
    for v, _ in tops[1:]:
        den = den + v
    for v, idx in tops:
        gates = gates + jnp.where(lane == idx, v / den, 0.0)
    o_ref[...] = gates


def _router(x, gain, router, *, tm):
    t, d = x.shape
    n_exp = router.shape[1]
    w = jnp.zeros((d, LANES), F32).at[:, :n_exp].set(router)
    return pl.pallas_call(
        functools.partial(_router_kernel, n_exp=n_exp),
        out_shape=jax.ShapeDtypeStruct((t, LANES), F32),
        grid=(t // tm,),
        in_specs=[pl.BlockSpec((tm, d), lambda i: (i, 0)),
                  pl.BlockSpec((1, d), lambda i: (0, 0)),
                  pl.BlockSpec((d, LANES), lambda i: (0, 0))],
        out_specs=pl.BlockSpec((tm, LANES), lambda i: (i, 0)),
        compiler_params=_params("parallel"),
        name="moe_router",
    )(x, gain.reshape(1, d), w)


def _final_norm_kernel(x_ref, g_ref, o_ref):
    o_ref[...] = _rms(x_ref[...], g_ref[...])


def _final_norm(x, gain, *, tm):
    t, d = x.shape
    return pl.pallas_call(
        _final_norm_kernel,
        out_shape=jax.ShapeDtypeStruct((t, d), F32),
        grid=(t // tm,),
        in_specs=[pl.BlockSpec((tm, d), lambda i: (i, 0)), pl.BlockSpec((1, d), lambda i: (0, 0))],
        out_specs=pl.BlockSpec((tm, d), lambda i: (i, 0)),
        compiler_params=_params("parallel"),
        name="final_norm",
    )(x, gain.reshape(1, d))


def _rope_tables(seq, dim):
    inv = 1.0 / (ROPE_THETA ** (jnp.arange(0, dim, 2, dtype=F32) / dim))
    ang = jnp.arange(seq, dtype=F32)[:, None] * inv[None, :]
    return jnp.cos(ang), jnp.sin(ang)


def _rot_half_cols(w):
    half = w.shape[-1] // 2
    return jnp.concatenate([w[..., half:], w[..., :half]], axis=-1)


def _tile(n, pref):
    for t in pref:
        if n % t == 0:
            return t
    return n


def kernel(x, attn_norm, w_in, w_out, cmp_k_pos, cmp_k_w1, cmp_k_w2, cmp_v_pos, cmp_v_w1, cmp_v_w2, mla_q_norm, mla_w_uq, mla_kv_norm, mla_w_ukv, ffn_norm, dense_w1, dense_w3, dense_w2, router, moe_w1, moe_w3, moe_w2, final_norm):
    b, s, d = x.shape
    depth = w_in.shape[0]
    t = b * s
    dh = LANES
    q_rank = mla_w_uq.shape[1]
    kv_rank = mla_w_ukv.shape[1]
    nsa_q = NSA_HEADS * dh
    nsa_kv = 6 * NSA_KV_HEADS * dh
    n_gate = 3 * NSA_HEADS
    o_q, o_kv, o_gate = 0, nsa_q, nsa_q + nsa_kv
    o_cq = o_gate + n_gate
    o_ckv = o_cq + q_rank
    o_kr = o_ckv + kv_rank
    assert o_kr + MLA_ROPE_DIM == w_in.shape[2]
    assert s % SLC_LEN == 0 and s % CMP_STRIDE == 0 and s // SLC_LEN <= LANES and s // CMP_STRIDE <= LANES
    c_cq = nsa_q + nsa_kv
    c_ckv = c_cq + q_rank
    c_kr = c_ckv + kv_rank
    c_gate = c_kr + 2 * MLA_ROPE_DIM
    n_in = c_gate + LANES
    assert q_rank % LANES == 0 and kv_rank % LANES == 0 and c_cq % q_rank == 0 and c_ckv % kv_rank == 0

    n_cmp = (s - CMP_LEN) // CMP_STRIDE + 1
    n_slc = s // SLC_LEN
    top_n = min(SLC_TOP, n_slc)

    cos_n, sin_n = _rope_tables(s, dh)
    cos_t = jnp.concatenate([cos_n, cos_n], axis=-1)
    sin_t = jnp.concatenate([-sin_n, sin_n], axis=-1)
    cos_m, sin_m = _rope_tables(s, MLA_ROPE_DIM)
    rot_cos = jnp.concatenate([cos_m, cos_m], axis=-1)
    rot_sin = jnp.concatenate([-sin_m, sin_m], axis=-1)
    mla_scale = (MLA_NOPE_DIM + MLA_ROPE_DIM) ** -0.5
    tab_q = jnp.concatenate([jnp.ones((s, MLA_NOPE_DIM), F32), rot_cos, rot_sin], axis=-1) * mla_scale
    tab_k = jnp.concatenate([rot_cos, rot_sin], axis=-1)

    cmp_tok = jnp.arange(LANES)[:, None] * CMP_STRIDE + jnp.arange(CMP_LEN)[None, :]
    overlap = jnp.mean(((cmp_tok[:, :, None] // SLC_LEN) == jnp.arange(LANES)[None, None, :]).astype(F32), axis=1)
    overlap = jnp.where(jnp.arange(LANES)[None, :] < n_slc, overlap, 0.0).astype(BF16)
    e_t = ((jnp.arange(s)[:, None] // SLC_LEN) == jnp.arange(LANES)[None, :]).astype(BF16)

    tm = _tile(t, (512, 256, 128))
    tq = _tile(s, (256, 128))
    hq = MLA_NOPE_DIM + 2 * MLA_ROPE_DIM

    xf = x.reshape(t, d)
    for i in range(depth):
        wi = w_in[i]
        kr_w = wi[:, o_kr:o_kr + MLA_ROPE_DIM]
        w_in2 = jnp.concatenate(
            [wi[:, o_q:o_q + nsa_q], wi[:, o_kv:o_kv + nsa_kv], wi[:, o_cq:o_cq + q_rank],
             wi[:, o_ckv:o_ckv + kv_rank], kr_w, _rot_half_cols(kr_w), wi[:, o_gate:o_gate + n_gate],
             jnp.zeros((d, LANES - n_gate), F32)], axis=1).astype(BF16)
        proj = _mm([xf], [w_in2], gain=attn_norm[i], tm=tm, tn=_tile(n_in, (512, 256, 128)), out_dtype=F32,
                   name="in_proj")
        proj3 = proj.reshape(b, s, n_in)

        qr, kc, vc, ks, vs, kw, vw = _nsa_prep(proj3, cos_t, sin_t, ts=_tile(s, (512, 256, 128)))
        k_cmp = _compress(kc, cmp_k_pos[i], cmp_k_w1[i], cmp_k_w2[i])
        v_cmp = _compress(vc, cmp_v_pos[i], cmp_v_w1[i], cmp_v_w2[i])
        o_cmp, sel = _cmp_sel(qr, k_cmp, v_cmp, overlap, tq=tq, n_cmp=n_cmp, n_slc=n_slc, top_n=top_n)
        o_nsa = _nsa_attn(qr, ks, vs, kw, vw, sel, e_t, o_cmp, proj3, c_gate // LANES, tq=tq, tk=tq)

        wq = mla_w_uq[i].reshape(q_rank, MLA_HEADS, MLA_NOPE_DIM + MLA_ROPE_DIM)
        wq_rope = wq[..., MLA_NOPE_DIM:]
        wq2 = jnp.concatenate([wq, _rot_half_cols(wq_rope)], axis=-1).reshape(q_rank, MLA_HEADS * hq).astype(BF16)
        q2 = _mm([proj], [wq2], x_cols=[c_cq // q_rank], gain=mla_q_norm[i], tab=tab_q, tm=tm, tn=hq,
                 out_dtype=BF16, name="mla_q_up")
        kv_up = _mm([proj], [mla_w_ukv[i].astype(BF16)], x_cols=[c_ckv // kv_rank], gain=mla_kv_norm[i],
                    tm=tm, tn=_tile(mla_w_ukv.shape[2], (512, 256, 128)), out_dtype=BF16, name="mla_kv_up")
        o_mla = _mla_attn(q2.reshape(b, s, -1), kv_up.reshape(b, s, -1), proj3, c_kr // LANES, tab_k, tq=tq, tk=tq)

        wo = w_out[i].astype(BF16)
        xf = _mm([o_nsa.reshape(t, -1), o_mla.reshape(t, -1)], [wo[:nsa_q], wo[nsa_q:]], res=xf, tm=tm,
                 tn=_tile(d, (512, 256, 128)), out_dtype=F32, name="out_proj")

        j = i // 2
        if i % 2 == 0:
            xf = _ffn(xf, ffn_norm[i], dense_w1[j:j + 1].astype(BF16), dense_w3[j:j + 1].astype(BF16),
                      dense_w2[j:j + 1].astype(BF16), tm=tm, tf=_tile(dense_w1.shape[2], (512, 256, 128)),
                      name="dense_ffn")
        else:
            gates = _router(xf, ffn_norm[i], router[j], tm=tm)
            n_exp = router.shape[2]
            gates_e = gates[:, :n_exp].T[:, :, None]
            xf = _ffn(xf, ffn_norm[i], moe_w1[j], moe_w3[j], moe_w2[j], gates_e, tm=tm,
                      tf=_tile(moe_w1.shape[3], (256, 128)), name="moe_ffn")
    return _final_norm(xf, final_norm, tm=tm).reshape(b, s, d)
```

```python
import functools

import jax
import jax.numpy as jnp
from jax import lax
from jax.experimental import pallas as pl
from jax.experimental.pallas import tpu as pltpu

F32 = jnp.float32
BF16 = jnp.bfloat16

NSA_HEADS = 8
NSA_KV_HEADS = 2
NSA_GROUP = NSA_HEADS // NSA_KV_HEADS
CMP_LEN = 32
CMP_STRIDE = 16
SLC_LEN = 64
SLC_SHIFT = 6
SLC_TOP = 16
N_FORCED_LOCAL = 2
FORCE_SCORE = 1.0e4
WINDOW = 512
MLA_HEADS = 8
MLA_NOPE_DIM = 128
MLA_ROPE_DIM = 64
MLA_V_DIM = 128
ROPE_THETA = 10000.0
RMS_EPS = 1e-6
TOP_K = 2

LANES = 128
NEG = -1.0e30
VMEM_LIMIT = 56 * 1024 * 1024


def _params(*sem):
    return pltpu.CompilerParams(dimension_semantics=sem, vmem_limit_bytes=VMEM_LIMIT)


def _rms(xf, g):
    return xf * lax.rsqrt(jnp.mean(xf * xf, axis=-1, keepdims=True) + RMS_EPS) * g


def _dot(a, b):
    return jnp.dot(a.astype(BF16), b.astype(BF16), preferred_element_type=F32)


def _dot_nt(a, b):
    return lax.dot_general(a.astype(BF16), b.astype(BF16), (((1,), (1,)), ((), ())),
                           preferred_element_type=F32)


def _mm_kernel(*refs, n_x, has_norm, has_tab, has_res):
    x_refs = refs[:n_x]
    w_refs = refs[n_x:2 * n_x]
    pos = 2 * n_x
    g_ref = tab_ref = res_ref = None
    if has_norm:
        g_ref = refs[pos]
        pos += 1
    if has_tab:
        tab_ref = refs[pos]
        pos += 1
    if has_res:
        res_ref = refs[pos]
        pos += 1
    o_ref = refs[pos]
    acc = None
    for x_ref, w_ref in zip(x_refs, w_refs):
        x = x_ref[...]
        if has_norm:
            x = _rms(x.astype(F32), g_ref[...])
        d = _dot(x, w_ref[...])
        acc = d if acc is None else acc + d
    if has_tab:
        acc = acc * tab_ref[...]
    if has_res:
        acc = acc + res_ref[...]
    o_ref[...] = acc.astype(o_ref.dtype)


def _mm(xs, ws, *, x_cols=None, gain=None, tab=None, res=None, tm, tn, out_dtype, name):
    n_x = len(xs)
    m = xs[0].shape[0]
    n = ws[0].shape[1]
    x_cols = x_cols or [0] * n_x
    in_specs, args = [], []
    for x, w, c in zip(xs, ws, x_cols):
        in_specs.append(pl.BlockSpec((tm, w.shape[0]), lambda i, j, c=c: (i, c)))
        args.append(x)
    for w in ws:
        in_specs.append(pl.BlockSpec((w.shape[0], tn), lambda i, j: (0, j)))
        args.append(w)
    if gain is not None:
        in_specs.append(pl.BlockSpec((1, gain.shape[-1]), lambda i, j: (0, 0)))
        args.append(gain.reshape(1, -1))
    if tab is not None:
        rb, cb = tab.shape[0] // tm, tab.shape[1] // tn
        in_specs.append(pl.BlockSpec((tm, tn), lambda i, j: (i % rb, j % cb)))
        args.append(tab)
    if res is not None:
        in_specs.append(pl.BlockSpec((tm, tn), lambda i, j: (i, j)))
        args.append(res)
    return pl.pallas_call(
        functools.partial(_mm_kernel, n_x=n_x, has_norm=gain is not None, has_tab=tab is not None,
                          has_res=res is not None),
        out_shape=jax.ShapeDtypeStruct((m, n), out_dtype),
        grid=(m // tm, n // tn),
        in_specs=in_specs,
        out_specs=pl.BlockSpec((tm, tn), lambda i, j: (i, j)),
        compiler_params=_params("parallel", "arbitrary"),
        name=name,
    )(*args)


def _nsa_prep_kernel(p_ref, cos_ref, sin_ref, q_ref, kc_ref, vc_ref, ks_ref, vs_ref, kw_ref, vw_ref):
    cos = cos_ref[...]
    sin = sin_ref[...]
    dh = LANES

    def rope(x):
        return x * cos + pltpu.roll(x, dh // 2, 1) * sin

    scale = dh ** -0.5
    for h in range(NSA_HEADS):
        q_ref[0, :, h * dh:(h + 1) * dh] = (rope(p_ref[0, :, h * dh:(h + 1) * dh]) * scale).astype(q_ref.dtype)
    base = NSA_HEADS * dh
    outs = (kc_ref, vc_ref, ks_ref, vs_ref, kw_ref, vw_ref)
    for t, o_ref in enumerate(outs):
        for g in range(NSA_KV_HEADS):
            c0 = base + (t * NSA_KV_HEADS + g) * dh
            v = p_ref[0, :, c0:c0 + dh]
            if t % 2 == 0:
                v = rope(v)
            o_ref[0, g] = v.astype(o_ref.dtype)


def _nsa_prep(proj, cos_t, sin_t, *, ts):
    b, s, _ = proj.shape
    g, dh = NSA_KV_HEADS, LANES
    width = (NSA_HEADS + 6 * g) * dh
    kv_spec = pl.BlockSpec((1, g, ts, dh), lambda bi, si: (bi, 0, si, 0))
    kv = lambda dt: jax.ShapeDtypeStruct((b, g, s, dh), dt)
    return pl.pallas_call(
        _nsa_prep_kernel,
        out_shape=(jax.ShapeDtypeStruct((b, s, NSA_HEADS * dh), BF16),
                   kv(F32), kv(F32), kv(BF16), kv(BF16), kv(BF16), kv(BF16)),
        grid=(b, s // ts),
        in_specs=[pl.BlockSpec((1, ts, width), lambda bi, si: (bi, si, 0)),
                  pl.BlockSpec((ts, dh), lambda bi, si: (si, 0)),
                  pl.BlockSpec((ts, dh), lambda bi, si: (si, 0))],
        out_specs=(pl.BlockSpec((1, ts, NSA_HEADS * dh), lambda bi, si: (bi, si, 0)),
                   kv_spec, kv_spec, kv_spec, kv_spec, kv_spec, kv_spec),
        compiler_params=_params("parallel", "parallel"),
        name="nsa_prep",
    )(proj, cos_t, sin_t)


def _compress_kernel(t_ref, pa_ref, pb_ref, w1a_ref, w1b_ref, w2_ref, o_ref):
    t = t_ref[0, 0]
    ya = _dot(t + pa_ref[...], w1a_ref[...])
    yb = _dot(t + pb_ref[...], w1b_ref[...])
    n = yb.shape[0]
    pre = ya + pltpu.roll(yb, n - 1, 0)
    o_ref[0, 0] = _dot(jax.nn.gelu(pre), w2_ref[...])


def _compress(t, pos_emb, w1, w2):
    b, g, s, dh = t.shape
    nb = s // CMP_STRIDE
    half = CMP_STRIDE * dh
    hid = w1.shape[1]
    t16 = t.reshape(b, g, nb, half)
    pa = pos_emb[:CMP_STRIDE].reshape(1, half)
    pb = pos_emb[CMP_STRIDE:].reshape(1, half)
    const = lambda shape: pl.BlockSpec(shape, lambda bi, gi: (0, 0))
    return pl.pallas_call(
        _compress_kernel,
        out_shape=jax.ShapeDtypeStruct((b, g, nb, dh), F32),
        grid=(b, g),
        in_specs=[pl.BlockSpec((1, 1, nb, half), lambda bi, gi: (bi, gi, 0, 0)),
                  const((1, half)), const((1, half)),
                  const((half, hid)), const((half, hid)), const((hid, dh))],
        out_specs=pl.BlockSpec((1, 1, nb, dh), lambda bi, gi: (bi, gi, 0, 0)),
        compiler_params=_params("parallel", "parallel"),
        name="nsa_compress",
    )(t16, pa, pb, w1[:half].astype(BF16), w1[half:].astype(BF16), w2.astype(BF16))


def _cmp_sel_kernel(q_ref, kc_ref, vc_ref, ov_ref, ocmp_ref, sel_ref, *, tq, n_cmp, n_slc, top_n):
    qi = pl.program_id(2)
    dh = LANES
    pos = qi * tq + lax.broadcasted_iota(jnp.int32, (tq, LANES), 0)
    lane = lax.broadcasted_iota(jnp.int32, (tq, LANES), 1)
    valid = (lane * CMP_STRIDE + (CMP_LEN - 1) <= pos) & (lane < n_cmp)
    kc = kc_ref[0, 0]
    vc = vc_ref[0, 0]
    psum = jnp.zeros((tq, LANES), F32)
    for r in range(NSA_GROUP):
        q = q_ref[0, :, r * dh:(r + 1) * dh]
        s = jnp.where(valid, _dot_nt(q, kc), NEG)
        m = jnp.max(s, axis=-1, keepdims=True)
        p = jnp.where(valid, jnp.exp(s - m), 0.0)
        p = p / jnp.maximum(jnp.sum(p, axis=-1, keepdims=True), 1e-30)
        ocmp_ref[0, :, r * dh:(r + 1) * dh] = _dot(p, vc)
        psum = psum + p
    hi = psum.astype(BF16)
    lo = psum - hi.astype(F32)
    imp = _dot(hi, ov_ref[...]) + _dot(lo, ov_ref[...])
    causal_blk = lane * SLC_LEN <= pos
    dist = lax.shift_right_logical(pos, SLC_SHIFT) - lane
    forced = (lane == 0) | ((dist >= 0) & (dist < N_FORCED_LOCAL))
    score = jnp.where(causal_blk, jnp.where(forced, FORCE_SCORE, imp), -jnp.inf)
    rank = jnp.zeros((tq, LANES), jnp.int32)
    for i in range(n_slc):
        col = score[:, i:i + 1]
        beats = (col > score) | ((col == score) & (lane > i))
        rank = rank + jnp.where(beats, 1, 0)
    sel = (rank < top_n) & (lane < n_slc)
    sel_ref[0, 0] = jnp.where(sel, 1.0, 0.0).astype(sel_ref.dtype)


def _cmp_sel(qr, k_cmp, v_cmp, overlap, *, tq, n_cmp, n_slc, top_n):
    b, s, _ = qr.shape
    g = NSA_KV_HEADS
    gw = NSA_GROUP * LANES
    nb = k_cmp.shape[2]
    return pl.pallas_call(
        functools.partial(_cmp_sel_kernel, tq=tq, n_cmp=n_cmp, n_slc=n_slc, top_n=top_n),
        out_shape=(jax.ShapeDtypeStruct((b, s, NSA_HEADS * LANES), F32),
                   jax.ShapeDtypeStruct((b, g, s, LANES), BF16)),
        grid=(b, g, s // tq),
        in_specs=[pl.BlockSpec((1, tq, gw), lambda bi, gi, qi: (bi, qi, gi)),
                  pl.BlockSpec((1, 1, nb, LANES), lambda bi, gi, qi: (bi, gi, 0, 0)),
                  pl.BlockSpec((1, 1, nb, LANES), lambda bi, gi, qi: (bi, gi, 0, 0)),
                  pl.BlockSpec((LANES, LANES), lambda bi, gi, qi: (0, 0))],
        out_specs=(pl.BlockSpec((1, tq, gw), lambda bi, gi, qi: (bi, qi, gi)),
                   pl.BlockSpec((1, 1, tq, LANES), lambda bi, gi, qi: (bi, gi, qi, 0))),
        compiler_params=_params("parallel", "parallel", "parallel"),
        name="nsa_cmp_select",
    )(qr, k_cmp, v_cmp, overlap)


def _flash_init(m_sc, l_sc, acc_sc):
    m_sc[...] = jnp.full(m_sc.shape, NEG, F32)
    l_sc[...] = jnp.zeros(l_sc.shape, F32)
    acc_sc[...] = jnp.zeros(acc_sc.shape, F32)


def _flash_step(q, k, v, mask, m_sc, l_sc, acc_sc, h):
    s = jnp.where(mask, _dot_nt(q, k), NEG)
    m_prev = m_sc[h]
    m_new = jnp.maximum(m_prev, jnp.max(s, axis=-1, keepdims=True))
    alpha = jnp.exp(m_prev - m_new)
    p = jnp.where(mask, jnp.exp(s - m_new), 0.0)
    l_sc[h] = alpha * l_sc[h] + jnp.sum(p, axis=-1, keepdims=True)
    acc_sc[h] = alpha * acc_sc[h] + _dot(p, v)
    m_sc[h] = m_new


def _flash_out(l_sc, acc_sc, h):
    return acc_sc[h] / jnp.maximum(l_sc[h], 1e-30)


def _nsa_attn_kernel(q_ref, ks_ref, vs_ref, kw_ref, vw_ref, sel_ref, et_ref, ocmp_ref, gate_ref, o_ref,
                     m_sc, l_sc, acc_sc, comb_sc, *, tq, tk):
    gi = pl.program_id(1)
    qi = pl.program_id(2)
    dh = LANES
    q0 = qi * tq
    qpos = q0 + lax.broadcasted_iota(jnp.int32, (tq, tk), 0)
    krel = lax.broadcasted_iota(jnp.int32, (tq, tk), 1)
    gates = jax.nn.sigmoid(gate_ref[0])
    n_gate = 3 * NSA_GROUP

    def gate_col(r, branch):
        col = None
        for g in range(NSA_KV_HEADS):
            c = g * n_gate + r * 3 + branch
            v = gates[:, c:c + 1]
            col = v if col is None else jnp.where(gi == g, v, col)
        return col

    _flash_init(m_sc, l_sc, acc_sc)
    sel = sel_ref[0, 0]

    def slc_body(j, carry):
        k0 = pl.multiple_of(j * tk, tk)
        chosen = _dot_nt(sel, et_ref[pl.ds(k0, tk), :])
        mask = (chosen > 0.5) & (k0 + krel <= qpos)
        k = ks_ref[0, 0, pl.ds(k0, tk), :]
        v = vs_ref[0, 0, pl.ds(k0, tk), :]
        for r in range(NSA_GROUP):
            _flash_step(q_ref[0, :, r * dh:(r + 1) * dh], k, v, mask, m_sc, l_sc, acc_sc, r)
        return carry

    lax.fori_loop(0, (q0 + tq + tk - 1) // tk, slc_body, 0)
    for r in range(NSA_GROUP):
        comb_sc[:, r * dh:(r + 1) * dh] = (gate_col(r, 0) * ocmp_ref[0, :, r * dh:(r + 1) * dh]
                                            + gate_col(r, 1) * _flash_out(l_sc, acc_sc, r))

    _flash_init(m_sc, l_sc, acc_sc)

    def win_body(j, carry):
        k0 = pl.multiple_of(j * tk, tk)
        diff = qpos - (k0 + krel)
        mask = (diff >= 0) & (diff < WINDOW)
        k = kw_ref[0, 0, pl.ds(k0, tk), :]
        v = vw_ref[0, 0, pl.ds(k0, tk), :]
        for r in range(NSA_GROUP):
            _flash_step(q_ref[0, :, r * dh:(r + 1) * dh], k, v, mask, m_sc, l_sc, acc_sc, r)
        return carry

    lax.fori_loop(jnp.maximum(q0 - (WINDOW - 1), 0) // tk, (q0 + tq + tk - 1) // tk, win_body, 0)
    for r in range(NSA_GROUP):
        o_ref[0, :, r * dh:(r + 1) * dh] = (comb_sc[:, r * dh:(r + 1) * dh]
                                            + gate_col(r, 2) * _flash_out(l_sc, acc_sc, r)).astype(o_ref.dtype)


def _nsa_attn(qr, ks, vs, kw, vw, sel, e_t, o_cmp, proj, gate_col_block, *, tq, tk):
    b, s, _ = qr.shape
    g = NSA_KV_HEADS
    gw = NSA_GROUP * LANES
    kv_spec = pl.BlockSpec((1, 1, s, LANES), lambda bi, gi, qi: (bi, gi, 0, 0))
    q_spec = pl.BlockSpec((1, tq, gw), lambda bi, gi, qi: (bi, qi, gi))
    return pl.pallas_call(
        functools.partial(_nsa_attn_kernel, tq=tq, tk=tk),
        out_shape=jax.ShapeDtypeStruct((b, s, NSA_HEADS * LANES), BF16),
        grid=(b, g, s // tq),
        in_specs=[q_spec, kv_spec, kv_spec, kv_spec, kv_spec,
                  pl.BlockSpec((1, 1, tq, LANES), lambda bi, gi, qi: (bi, gi, qi, 0)),
                  pl.BlockSpec((s, LANES), lambda bi, gi, qi: (0, 0)),
                  q_spec,
                  pl.BlockSpec((1, tq, LANES), lambda bi, gi, qi: (bi, qi, gate_col_block))],
        out_specs=q_spec,
        scratch_shapes=[pltpu.VMEM((NSA_GROUP, tq, 1), F32), pltpu.VMEM((NSA_GROUP, tq, 1), F32),
                        pltpu.VMEM((NSA_GROUP, tq, LANES), F32), pltpu.VMEM((tq, gw), F32)],
        compiler_params=_params("parallel", "parallel", "parallel"),
        name="nsa_slc_win",
    )(qr, ks, vs, kw, vw, sel, e_t, o_cmp, proj)


def _mla_attn_kernel(q_ref, kn_ref, v_ref, kr_ref, tk_ref, o_ref, kcat_sc, m_sc, l_sc, acc_sc, *, tq, tk):
    qi = pl.program_id(2)
    nope = MLA_NOPE_DIM

    @pl.when(qi == 0)
    def _():
        prod = kr_ref[0] * tk_ref[...]
        rope2 = prod + pltpu.roll(prod, MLA_ROPE_DIM, 1)
        kcat_sc[:, :nope] = kn_ref[0]
        kcat_sc[:, nope:] = rope2.astype(kcat_sc.dtype)

    q0 = qi * tq
    qpos = q0 + lax.broadcasted_iota(jnp.int32, (tq, tk), 0)
    krel = lax.broadcasted_iota(jnp.int32, (tq, tk), 1)
    _flash_init(m_sc, l_sc, acc_sc)
    q = q_ref[0]

    def body(j, carry):
        k0 = pl.multiple_of(j * tk, tk)
        mask = k0 + krel <= qpos
        _flash_step(q, kcat_sc[pl.ds(k0, tk), :], v_ref[0, pl.ds(k0, tk), :], mask, m_sc, l_sc, acc_sc, 0)
        return carry

    lax.fori_loop(0, (q0 + tq + tk - 1) // tk, body, 0)
    o_ref[0] = _flash_out(l_sc, acc_sc, 0).astype(o_ref.dtype)


def _mla_attn(q2, kv_up, proj, kr_col_block, tab_k, *, tq, tk):
    b, s, _ = q2.shape
    hq = MLA_NOPE_DIM + 2 * MLA_ROPE_DIM
    return pl.pallas_call(
        functools.partial(_mla_attn_kernel, tq=tq, tk=tk),
        out_shape=jax.ShapeDtypeStruct((b, s, MLA_HEADS * MLA_V_DIM), BF16),
        grid=(b, MLA_HEADS, s // tq),
        in_specs=[pl.BlockSpec((1, tq, hq), lambda bi, hi, qi: (bi, qi, hi)),
                  pl.BlockSpec((1, s, MLA_NOPE_DIM), lambda bi, hi, qi: (bi, 0, 2 * hi)),
                  pl.BlockSpec((1, s, MLA_V_DIM), lambda bi, hi, qi: (bi, 0, 2 * hi + 1)),
                  pl.BlockSpec((1, s, LANES), lambda bi, hi, qi: (bi, 0, kr_col_block)),
                  pl.BlockSpec((s, LANES), lambda bi, hi, qi: (0, 0))],
        out_specs=pl.BlockSpec((1, tq, MLA_V_DIM), lambda bi, hi, qi: (bi, qi, hi)),
        scratch_shapes=[pltpu.VMEM((s, hq), BF16), pltpu.VMEM((1, tq, 1), F32), pltpu.VMEM((1, tq, 1), F32),
                        pltpu.VMEM((1, tq, MLA_V_DIM), F32)],
        compiler_params=_params("parallel", "parallel", "arbitrary"),
        name="mla_attn",
    )(q2, kv_up, kv_up, proj, tab_k)


def _ffn_kernel(x_ref, g_ref, w1_ref, w3_ref, w2_ref, o_ref, hn_sc, acc_sc):
    f = pl.program_id(1)

    @pl.when(f == 0)
    def _():
        hn_sc[...] = _rms(x_ref[...], g_ref[...]).astype(hn_sc.dtype)
        acc_sc[...] = jnp.zeros(acc_sc.shape, F32)

    hn = hn_sc[...]
    z = jax.nn.silu(_dot(hn, w1_ref[...])) * _dot(hn, w3_ref[...])
    acc_sc[...] += _dot(z, w2_ref[...])

    @pl.when(f == pl.num_programs(1) - 1)
    def _():
        o_ref[...] = x_ref[...] + acc_sc[...]


def _ffn(x, gain, w1, w3, w2, *, tm, tf):
    t, d = x.shape
    ff = w1.shape[1]
    return pl.pallas_call(
        _ffn_kernel,
        out_shape=jax.ShapeDtypeStruct((t, d), F32),
        grid=(t // tm, ff // tf),
        in_specs=[pl.BlockSpec((tm, d), lambda i, f: (i, 0)),
                  pl.BlockSpec((1, d), lambda i, f: (0, 0)),
                  pl.BlockSpec((d, tf), lambda i, f: (0, f)),
                  pl.BlockSpec((d, tf), lambda i, f: (0, f)),
                  pl.BlockSpec((tf, d), lambda i, f: (f, 0))],
        out_specs=pl.BlockSpec((tm, d), lambda i, f: (i, 0)),
        scratch_shapes=[pltpu.VMEM((tm, d), BF16), pltpu.VMEM((tm, d), F32)],
        compiler_params=_params("parallel", "arbitrary"),
        name="dense_ffn",
    )(x, gain.reshape(1, d), w1, w3, w2)


def _row_copy(src_ref, dst_ref, sem, src_row, dst_row):
    return pltpu.make_async_copy(src_ref.at[pl.ds(src_row, 1)], dst_ref.at[pl.ds(dst_row, 1)], sem)


def _row_gather_kernel(idx_ref, src_ref, out_ref, sem, *, chunk):
    base = pl.program_id(0) * chunk

    def issue(r, carry):
        _row_copy(src_ref, out_ref, sem, idx_ref[r], base + r).start()
        return carry

    def drain(r, carry):
        _row_copy(src_ref, out_ref, sem, idx_ref[r], base + r).wait()
        return carry

    lax.fori_loop(0, chunk, issue, 0, unroll=8)
    lax.fori_loop(0, chunk, drain, 0, unroll=8)


def _row_gather(src, idx, *, chunk, name):
    m = idx.shape[0]
    return pl.pallas_call(
        functools.partial(_row_gather_kernel, chunk=chunk),
        out_shape=jax.ShapeDtypeStruct((m, src.shape[1]), src.dtype),
        grid=(m // chunk,),
        in_specs=[pl.BlockSpec((chunk,), lambda i: (i,), memory_space=pltpu.SMEM),
                  pl.BlockSpec(memory_space=pl.ANY)],
        out_specs=pl.BlockSpec(memory_space=pl.ANY),
        scratch_shapes=[pltpu.SemaphoreType.DMA(())],
        compiler_params=_params("arbitrary"),
        name=name,
    )(idx, src)


def _moe_ffn_kernel(te_ref, tl_ref, x_ref, g_ref, w1_ref, w3_ref, w2_ref, y_ref, hn_sc):
    i = pl.program_id(0)
    f = pl.program_id(1)
    live = tl_ref[i] > 0

    @pl.when(f == 0)
    def _():
        y_ref[...] = jnp.zeros(y_ref.shape, F32)

    @pl.when(live)
    def _():
        @pl.when(f == 0)
        def _():
            hn_sc[...] = _rms(x_ref[...], g_ref[...]).astype(hn_sc.dtype)

        hn = hn_sc[...]
        z = jax.nn.silu(_dot(hn, w1_ref[0])) * _dot(hn, w3_ref[0])
        y_ref[...] += _dot(z, w2_ref[0])


def _moe_ffn(x_sorted, gain, w1, w3, w2, tile_expert, tile_live, *, tm, tf):
    p, d = x_sorted.shape
    ff = w1.shape[2]

    def w_col(i, f, te, tl):
        return (te[i], 0, jnp.where(tl[i] > 0, f, 0))

    def w_row(i, f, te, tl):
        return (te[i], jnp.where(tl[i] > 0, f, 0), 0)

    return pl.pallas_call(
        _moe_ffn_kernel,
        out_shape=jax.ShapeDtypeStruct((p, d), F32),
        grid_spec=pltpu.PrefetchScalarGridSpec(
            num_scalar_prefetch=2,
            grid=(p // tm, ff // tf),
            in_specs=[pl.BlockSpec((tm, d), lambda i, f, te, tl: (i, 0)),
                      pl.BlockSpec((1, d), lambda i, f, te, tl: (0, 0)),
                      pl.BlockSpec((1, d, tf), w_col),
                      pl.BlockSpec((1, d, tf), w_col),
                      pl.BlockSpec((1, tf, d), w_row)],
            out_specs=pl.BlockSpec((tm, d), lambda i, f, te, tl: (i, 0)),
            scratch_shapes=[pltpu.VMEM((tm, d), BF16)]),
        compiler_params=_params("parallel", "arbitrary"),
        name="moe_ffn",
    )(tile_expert, tile_live, x_sorted, gain.reshape(1, d), w1, w3, w2)


def _moe_combine_kernel(*refs, final):
    if final:
        x_ref, y_ref, r_ref, g_ref, o_ref = refs
    else:
        x_ref, y_ref, r_ref, o_ref = refs
    route = r_ref[...]
    mix = None
    for k in range(TOP_K):
        term = route[:, TOP_K + k:TOP_K + k + 1] * y_ref[k]
        mix = term if mix is None else mix + term
    out = x_ref[...] + mix
    if final:
        out = _rms(out, g_ref[...])
    o_ref[...] = out


def _moe_combine(x, y_k, route, final_gain, *, tm):
    t, d = x.shape
    in_specs = [pl.BlockSpec((tm, d), lambda i: (i, 0)),
                pl.BlockSpec((TOP_K, tm, d), lambda i: (0, i, 0)),
                pl.BlockSpec((tm, LANES), lambda i: (i, 0))]
    args = [x, y_k, route]
    if final_gain is not None:
        in_specs.append(pl.BlockSpec((1, d), lambda i: (0, 0)))
        args.append(final_gain.reshape(1, d))
    return pl.pallas_call(
        functools.partial(_moe_combine_kernel, final=final_gain is not None),
        out_shape=jax.ShapeDtypeStruct((t, d), F32),
        grid=(t // tm,),
        in_specs=in_specs,
        out_specs=pl.BlockSpec((tm, d), lambda i: (i, 0)),
        compiler_params=_params("parallel"),
        name="moe_combine",
    )(*args)


def _router_kernel(x_ref, g_ref, w_ref, o_ref, *, n_exp):
    hn = _rms(x_ref[...], g_ref[...])
    w = w_ref[...]
    h_hi = hn.astype(BF16)
    h_lo = hn - h_hi.astype(F32)
    w_hi = w.astype(BF16)
    w_lo = w - w_hi.astype(F32)
    logits = _dot(h_hi, w_hi) + _dot(h_hi, w_lo) + _dot(h_lo, w_hi)
    lane = lax.broadcasted_iota(jnp.int32, logits.shape, 1)
    valid = lane < n_exp
    lg = jnp.where(valid, logits, NEG)
    ex = jnp.where(valid, jnp.exp(lg - jnp.max(lg, axis=-1, keepdims=True)), 0.0)
    probs = ex / jnp.sum(ex, axis=-1, keepdims=True)
    rest = jnp.where(valid, probs, -1.0)
    tops = []
    for _ in range(TOP_K):
        v = jnp.max(rest, axis=-1, keepdims=True)
        idx = jnp.min(jnp.where(rest == v, lane, LANES), axis=-1, keepdims=True)
        tops.append((v, idx))
        rest = jnp.where(lane == idx, -1.0, rest)
    den = tops[0][0]
    for v, _ in tops[1:]:
        den = den + v
    route = jnp.zeros(logits.shape, F32)
    for k, (v, idx) in enumerate(tops):
        route = jnp.where(lane == k, idx.astype(F32), route)
        route = jnp.where(lane == TOP_K + k, v / den, route)
    o_ref[...] = route


def _router(x, gain, router, *, tm):
    t, d = x.shape
    n_exp = router.shape[1]
    w = jnp.zeros((d, LANES), F32).at[:, :n_exp].set(router)
    return pl.pallas_call(
        functools.partial(_router_kernel, n_exp=n_exp),
        out_shape=jax.ShapeDtypeStruct((t, LANES), F32),
        grid=(t // tm,),
        in_specs=[pl.BlockSpec((tm, d), lambda i: (i, 0)),
                  pl.BlockSpec((1, d), lambda i: (0, 0)),
                  pl.BlockSpec((d, LANES), lambda i: (0, 0))],
        out_specs=pl.BlockSpec((tm, LANES), lambda i: (i, 0)),
        compiler_params=_params("parallel"),
        name="moe_router",
    )(x, gain.reshape(1, d), w)


def _final_norm_kernel(x_ref, g_ref, o_ref):
    o_ref[...] = _rms(x_ref[...], g_ref[...])


def _final_norm(x, gain, *, tm):
    t, d = x.shape
    return pl.pallas_call(
        _final_norm_kernel,
        out_shape=jax.ShapeDtypeStruct((t, d), F32),
        grid=(t // tm,),
        in_specs=[pl.BlockSpec((tm, d), lambda i: (i, 0)), pl.BlockSpec((1, d), lambda i: (0, 0))],
        out_specs=pl.BlockSpec((tm, d), lambda i: (i, 0)),
        compiler_params=_params("parallel"),
        name="final_norm",
    )(x, gain.reshape(1, d))


def _rope_tables(seq, dim):
    inv = 1.0 / (ROPE_THETA ** (jnp.arange(0, dim, 2, dtype=F32) / dim))
    ang = jnp.arange(seq, dtype=F32)[:, None] * inv[None, :]
    return jnp.cos(ang), jnp.sin(ang)


def _rot_half_cols(w):
    half = w.shape[-1] // 2
    return jnp.concatenate([w[..., half:], w[..., :half]], axis=-1)


def _tile(n, pref):
    for t in pref:
        if n % t == 0:
            return t
    return n


def kernel(x, attn_norm, w_in, w_out, cmp_k_pos, cmp_k_w1, cmp_k_w2, cmp_v_pos, cmp_v_w1, cmp_v_w2, mla_q_norm, mla_w_uq, mla_kv_norm, mla_w_ukv, ffn_norm, dense_w1, dense_w3, dense_w2, router, moe_w1, moe_w3, moe_w2, final_norm):
    b, s, d = x.shape
    depth = w_in.shape[0]
    t = b * s
    dh = LANES
    q_rank = mla_w_uq.shape[1]
    kv_rank = mla_w_ukv.shape[1]
    nsa_q = NSA_HEADS * dh
    nsa_kv = 6 * NSA_KV_HEADS * dh
    n_gate = 3 * NSA_HEADS
    o_q, o_kv, o_gate = 0, nsa_q, nsa_q + nsa_kv
    o_cq = o_gate + n_gate
    o_ckv = o_cq + q_rank
    o_kr = o_ckv + kv_rank
    assert o_kr + MLA_ROPE_DIM == w_in.shape[2]
    assert s % SLC_LEN == 0 and s % CMP_STRIDE == 0 and s // SLC_LEN <= LANES and s // CMP_STRIDE <= LANES
    c_cq = nsa_q + nsa_kv
    c_ckv = c_cq + q_rank
    c_kr = c_ckv + kv_rank
    c_gate = c_kr + 2 * MLA_ROPE_DIM
    n_in = c_gate + LANES
    assert q_rank % LANES == 0 and kv_rank % LANES == 0 and c_cq % q_rank == 0 and c_ckv % kv_rank == 0

    n_cmp = (s - CMP_LEN) // CMP_STRIDE + 1
    n_slc = s // SLC_LEN
    top_n = min(SLC_TOP, n_slc)

    cos_n, sin_n = _rope_tables(s, dh)
    cos_t = jnp.concatenate([cos_n, cos_n], axis=-1)
    sin_t = jnp.concatenate([-sin_n, sin_n], axis=-1)
    cos_m, sin_m = _rope_tables(s, MLA_ROPE_DIM)
    rot_cos = jnp.concatenate([cos_m, cos_m], axis=-1)
    rot_sin = jnp.concatenate([-sin_m, sin_m], axis=-1)
    mla_scale = (MLA_NOPE_DIM + MLA_ROPE_DIM) ** -0.5
    tab_q = jnp.concatenate([jnp.ones((s, MLA_NOPE_DIM), F32), rot_cos, rot_sin], axis=-1) * mla_scale
    tab_k = jnp.concatenate([rot_cos, rot_sin], axis=-1)

    cmp_tok = jnp.arange(LANES)[:, None] * CMP_STRIDE + jnp.arange(CMP_LEN)[None, :]
    overlap = jnp.mean(((cmp_tok[:, :, None] // SLC_LEN) == jnp.arange(LANES)[None, None, :]).astype(F32), axis=1)
    overlap = jnp.where(jnp.arange(LANES)[None, :] < n_slc, overlap, 0.0).astype(BF16)
    e_t = ((jnp.arange(s)[:, None] // SLC_LEN) == jnp.arange(LANES)[None, :]).astype(BF16)

    tm = _tile(t, (512, 256, 128))
    tq = _tile(s, (256, 128))
    hq = MLA_NOPE_DIM + 2 * MLA_ROPE_DIM

    xf = x.reshape(t, d)
    for i in range(depth):
        wi = w_in[i]
        kr_w = wi[:, o_kr:o_kr + MLA_ROPE_DIM]
        w_in2 = jnp.concatenate(
            [wi[:, o_q:o_q + nsa_q], wi[:, o_kv:o_kv + nsa_kv], wi[:, o_cq:o_cq + q_rank],
             wi[:, o_ckv:o_ckv + kv_rank], kr_w, _rot_half_cols(kr_w), wi[:, o_gate:o_gate + n_gate],
             jnp.zeros((d, LANES - n_gate), F32)], axis=1).astype(BF16)
        proj = _mm([xf], [w_in2], gain=attn_norm[i], tm=tm, tn=_tile(n_in, (512, 256, 128)), out_dtype=F32,
                   name="in_proj")
        proj3 = proj.reshape(b, s, n_in)

        qr, kc, vc, ks, vs, kw, vw = _nsa_prep(proj3, cos_t, sin_t, ts=_tile(s, (512, 256, 128)))
        k_cmp = _compress(kc, cmp_k_pos[i], cmp_k_w1[i], cmp_k_w2[i])
        v_cmp = _compress(vc, cmp_v_pos[i], cmp_v_w1[i], cmp_v_w2[i])
        o_cmp, sel = _cmp_sel(qr, k_cmp, v_cmp, overlap, tq=tq, n_cmp=n_cmp, n_slc=n_slc, top_n=top_n)
        o_nsa = _nsa_attn(qr, ks, vs, kw, vw, sel, e_t, o_cmp, proj3, c_gate // LANES, tq=tq, tk=tq)

        wq = mla_w_uq[i].reshape(q_rank, MLA_HEADS, MLA_NOPE_DIM + MLA_ROPE_DIM)
        wq_rope = wq[..., MLA_NOPE_DIM:]
        wq2 = jnp.concatenate([wq, _rot_half_cols(wq_rope)], axis=-1).reshape(q_rank, MLA_HEADS * hq).astype(BF16)
        q2 = _mm([proj], [wq2], x_cols=[c_cq // q_rank], gain=mla_q_norm[i], tab=tab_q, tm=tm, tn=hq,
                 out_dtype=BF16, name="mla_q_up")
        kv_up = _mm([proj], [mla_w_ukv[i].astype(BF16)], x_cols=[c_ckv // kv_rank], gain=mla_kv_norm[i],
                    tm=tm, tn=_tile(mla_w_ukv.shape[2], (512, 256, 128)), out_dtype=BF16, name="mla_kv_up")
        o_mla = _mla_attn(q2.reshape(b, s, -1), kv_up.reshape(b, s, -1), proj3, c_kr // LANES, tab_k, tq=tq, tk=tq)

        wo = w_out[i].astype(BF16)
        xf = _mm([o_nsa.reshape(t, -1), o_mla.reshape(t, -1)], [wo[:nsa_q], wo[nsa_q:]], res=xf, tm=tm,
                 tn=_tile(d, (512, 256, 128)), out_dtype=F32, name="out_proj")

        j = i // 2
        if i % 2 == 0:
            xf = _ffn(xf, ffn_norm[i], dense_w1[j].astype(BF16), dense_w3[j].astype(BF16),
                      dense_w2[j].astype(BF16), tm=tm, tf=_tile(dense_w1.shape[2], (512, 256, 128)))
        else:
            route = _router(xf, ffn_norm[i], router[j], tm=tm)
            n_exp = router.shape[2]
            pair_expert = route[:, :TOP_K].astype(jnp.int32).reshape(-1)
            onehot = (pair_expert[:, None] == jnp.arange(n_exp)[None, :]).astype(jnp.int32)
            csum = jnp.cumsum(onehot, axis=0)
            rank = jnp.sum(onehot * (csum - 1), axis=1)
            group_end = jnp.cumsum((csum[-1] + tm - 1) // tm * tm)
            group_start = group_end - (csum[-1] + tm - 1) // tm * tm
            slot = (group_start[pair_expert] + rank).astype(jnp.int32)
            n_tiles = TOP_K * t // tm + n_exp
            tile_start = jnp.arange(n_tiles) * tm
            tile_expert = jnp.minimum(jnp.sum(tile_start[:, None] >= group_end[None, :], axis=1),
                                      n_exp - 1).astype(jnp.int32)
            tile_live = (tile_start < group_end[-1]).astype(jnp.int32)
            pair_token = jnp.arange(TOP_K * t, dtype=jnp.int32) // TOP_K
            slot_token = jnp.zeros((n_tiles * tm,), jnp.int32).at[slot].set(pair_token)
            x_sorted = _row_gather(xf, slot_token, chunk=tm, name="moe_gather")
            y_sorted = _moe_ffn(x_sorted, ffn_norm[i], moe_w1[j].astype(BF16), moe_w3[j].astype(BF16),
                                moe_w2[j].astype(BF16), tile_expert, tile_live, tm=tm,
                                tf=_tile(moe_w1.shape[3], (512, 256, 128)))
            y_k = _row_gather(y_sorted, slot.reshape(t, TOP_K).T.reshape(-1), chunk=tm, name="moe_ungather")
            last = i == depth - 1
            xf = _moe_combine(xf, y_k.reshape(TOP_K, t, d), route, final_norm if last else None, tm=tm)
            if last:
                return xf.reshape(b, s, d)
    return _final_norm(xf, final_norm, tm=tm).reshape(b, s, d)
```

```python
import functools

import jax
import jax.numpy as jnp
from jax import lax
from jax.experimental import pallas as pl
from jax.experimental.pallas import tpu as pltpu

F32 = jnp.float32
BF16 = jnp.bfloat16

NSA_HEADS = 8
NSA_KV_HEADS = 2
NSA_GROUP = NSA_HEADS // NSA_KV_HEADS
CMP_LEN = 32
CMP_STRIDE = 16
SLC_LEN = 64
SLC_SHIFT = 6
SLC_TOP = 16
N_FORCED_LOCAL = 2
FORCE_SCORE = 1.0e4
WINDOW = 512
MLA_HEADS = 8
MLA_NOPE_DIM = 128
MLA_ROPE_DIM = 64
MLA_V_DIM = 128
ROPE_THETA = 10000.0
RMS_EPS = 1e-6
TOP_K = 2

LANES = 128
NEG = -1.0e30
VMEM_LIMIT = 56 * 1024 * 1024


def _params(*sem):
    return pltpu.CompilerParams(dimension_semantics=sem, vmem_limit_bytes=VMEM_LIMIT)


def _rms(xf, g):
    return xf * lax.rsqrt(jnp.mean(xf * xf, axis=-1, keepdims=True) + RMS_EPS) * g


def _dot(a, b):
    return jnp.dot(a.astype(BF16), b.astype(BF16), preferred_element_type=F32)


def _dot_nt(a, b):
    return lax.dot_general(a.astype(BF16), b.astype(BF16), (((1,), (1,)), ((), ())),
                           preferred_element_type=F32)


def _mm_kernel(*refs, n_x, has_norm, has_tab, has_res):
    x_refs = refs[:n_x]
    w_refs = refs[n_x:2 * n_x]
    pos = 2 * n_x
    g_ref = tab_ref = res_ref = None
    if has_norm:
        g_ref = refs[pos]
        pos += 1
    if has_tab:
        tab_ref = refs[pos]
        pos += 1
    if has_res:
        res_ref = refs[pos]
        pos += 1
    o_ref = refs[pos]
    acc = None
    for x_ref, w_ref in zip(x_refs, w_refs):
        x = x_ref[...]
        if has_norm:
            x = _rms(x.astype(F32), g_ref[...])
        d = _dot(x, w_ref[...])
        acc = d if acc is None else acc + d
    if has_tab:
        acc = acc * tab_ref[...]
    if has_res:
        acc = acc + res_ref[...]
    o_ref[...] = acc.astype(o_ref.dtype)


def _mm(xs, ws, *, x_cols=None, gain=None, tab=None, res=None, tm, tn, out_dtype, name):
    n_x = len(xs)
    m = xs[0].shape[0]
    n = ws[0].shape[1]
    x_cols = x_cols or [0] * n_x
    in_specs, args = [], []
    for x, w, c in zip(xs, ws, x_cols):
        in_specs.append(pl.BlockSpec((tm, w.shape[0]), lambda i, j, c=c: (i, c)))
        args.append(x)
    for w in ws:
        in_specs.append(pl.BlockSpec((w.shape[0], tn), lambda i, j: (0, j)))
        args.append(w)
    if gain is not None:
        in_specs.append(pl.BlockSpec((1, gain.shape[-1]), lambda i, j: (0, 0)))
        args.append(gain.reshape(1, -1))
    if tab is not None:
        rb, cb = tab.shape[0] // tm, tab.shape[1] // tn
        in_specs.append(pl.BlockSpec((tm, tn), lambda i, j: (i % rb, j % cb)))
        args.append(tab)
    if res is not None:
        in_specs.append(pl.BlockSpec((tm, tn), lambda i, j: (i, j)))
        args.append(res)
    return pl.pallas_call(
        functools.partial(_mm_kernel, n_x=n_x, has_norm=gain is not None, has_tab=tab is not None,
                          has_res=res is not None),
        out_shape=jax.ShapeDtypeStruct((m, n), out_dtype),
        grid=(m // tm, n // tn),
        in_specs=in_specs,
        out_specs=pl.BlockSpec((tm, tn), lambda i, j: (i, j)),
        compiler_params=_params("parallel", "arbitrary"),
        name=name,
    )(*args)


def _nsa_prep_kernel(p_ref, cos_ref, sin_ref, q_ref, kc_ref, vc_ref, ks_ref, vs_ref, kw_ref, vw_ref):
    cos = cos_ref[...]
    sin = sin_ref[...]
    dh = LANES

    def rope(x):
        return x * cos + pltpu.roll(x, dh // 2, 1) * sin

    scale = dh ** -0.5
    for h in range(NSA_HEADS):
        q_ref[0, :, h * dh:(h + 1) * dh] = (rope(p_ref[0, :, h * dh:(h + 1) * dh]) * scale).astype(q_ref.dtype)
    base = NSA_HEADS * dh
    outs = (kc_ref, vc_ref, ks_ref, vs_ref, kw_ref, vw_ref)
    for t, o_ref in enumerate(outs):
        for g in range(NSA_KV_HEADS):
            c0 = base + (t * NSA_KV_HEADS + g) * dh
            v = p_ref[0, :, c0:c0 + dh]
            if t % 2 == 0:
                v = rope(v)
            o_ref[0, g] = v.astype(o_ref.dtype)


def _nsa_prep(proj, cos_t, sin_t, *, ts):
    b, s, _ = proj.shape
    g, dh = NSA_KV_HEADS, LANES
    width = (NSA_HEADS + 6 * g) * dh
    kv_spec = pl.BlockSpec((1, g, ts, dh), lambda bi, si: (bi, 0, si, 0))
    kv = lambda dt: jax.ShapeDtypeStruct((b, g, s, dh), dt)
    return pl.pallas_call(
        _nsa_prep_kernel,
        out_shape=(jax.ShapeDtypeStruct((b, s, NSA_HEADS * dh), BF16),
                   kv(F32), kv(F32), kv(BF16), kv(BF16), kv(BF16), kv(BF16)),
        grid=(b, s // ts),
        in_specs=[pl.BlockSpec((1, ts, width), lambda bi, si: (bi, si, 0)),
                  pl.BlockSpec((ts, dh), lambda bi, si: (si, 0)),
                  pl.BlockSpec((ts, dh), lambda bi, si: (si, 0))],
        out_specs=(pl.BlockSpec((1, ts, NSA_HEADS * dh), lambda bi, si: (bi, si, 0)),
                   kv_spec, kv_spec, kv_spec, kv_spec, kv_spec, kv_spec),
        compiler_params=_params("parallel", "parallel"),
        name="nsa_prep",
    )(proj, cos_t, sin_t)


def _compress_kernel(t_ref, pa_ref, pb_ref, w1a_ref, w1b_ref, w2_ref, o_ref):
    t = t_ref[0, 0]
    ya = _dot(t + pa_ref[...], w1a_ref[...])
    yb = _dot(t + pb_ref[...], w1b_ref[...])
    n = yb.shape[0]
    pre = ya + pltpu.roll(yb, n - 1, 0)
    o_ref[0, 0] = _dot(jax.nn.gelu(pre), w2_ref[...])


def _compress(t, pos_emb, w1, w2):
    b, g, s, dh = t.shape
    nb = s // CMP_STRIDE
    half = CMP_STRIDE * dh
    hid = w1.shape[1]
    t16 = t.reshape(b, g, nb, half)
    pa = pos_emb[:CMP_STRIDE].reshape(1, half)
    pb = pos_emb[CMP_STRIDE:].reshape(1, half)
    const = lambda shape: pl.BlockSpec(shape, lambda bi, gi: (0, 0))
    return pl.pallas_call(
        _compress_kernel,
        out_shape=jax.ShapeDtypeStruct((b, g, nb, dh), F32),
        grid=(b, g),
        in_specs=[pl.BlockSpec((1, 1, nb, half), lambda bi, gi: (bi, gi, 0, 0)),
                  const((1, half)), const((1, half)),
                  const((half, hid)), const((half, hid)), const((hid, dh))],
        out_specs=pl.BlockSpec((1, 1, nb, dh), lambda bi, gi: (bi, gi, 0, 0)),
        compiler_params=_params("parallel", "parallel"),
        name="nsa_compress",
    )(t16, pa, pb, w1[:half].astype(BF16), w1[half:].astype(BF16), w2.astype(BF16))


def _cmp_sel_kernel(q_ref, kc_ref, vc_ref, ov_ref, ocmp_ref, sel_ref, *, tq, n_cmp, n_slc, top_n):
    qi = pl.program_id(2)
    dh = LANES
    pos = qi * tq + lax.broadcasted_iota(jnp.int32, (tq, LANES), 0)
    lane = lax.broadcasted_iota(jnp.int32, (tq, LANES), 1)
    valid = (lane * CMP_STRIDE + (CMP_LEN - 1) <= pos) & (lane < n_cmp)
    kc = kc_ref[0, 0]
    vc = vc_ref[0, 0]
    psum = jnp.zeros((tq, LANES), F32)
    for r in range(NSA_GROUP):
        q = q_ref[0, :, r * dh:(r + 1) * dh]
        s = jnp.where(valid, _dot_nt(q, kc), NEG)
        m = jnp.max(s, axis=-1, keepdims=True)
        p = jnp.where(valid, jnp.exp(s - m), 0.0)
        p = p / jnp.maximum(jnp.sum(p, axis=-1, keepdims=True), 1e-30)
        ocmp_ref[0, :, r * dh:(r + 1) * dh] = _dot(p, vc)
        psum = psum + p
    hi = psum.astype(BF16)
    lo = psum - hi.astype(F32)
    imp = _dot(hi, ov_ref[...]) + _dot(lo, ov_ref[...])
    causal_blk = lane * SLC_LEN <= pos
    dist = lax.shift_right_logical(pos, SLC_SHIFT) - lane
    forced = (lane == 0) | ((dist >= 0) & (dist < N_FORCED_LOCAL))
    score = jnp.where(causal_blk, jnp.where(forced, FORCE_SCORE, imp), -jnp.inf)
    rank = jnp.zeros((tq, LANES), jnp.int32)
    for i in range(n_slc):
        col = score[:, i:i + 1]
        beats = (col > score) | ((col == score) & (lane > i))
        rank = rank + jnp.where(beats, 1, 0)
    sel_ref[0, 0] = jnp.where((rank < top_n) | (lane >= n_slc), 0.0, NEG).astype(sel_ref.dtype)


def _cmp_sel(qr, k_cmp, v_cmp, overlap, *, tq, n_cmp, n_slc, top_n):
    b, s, _ = qr.shape
    g = NSA_KV_HEADS
    gw = NSA_GROUP * LANES
    nb = k_cmp.shape[2]
    return pl.pallas_call(
        functools.partial(_cmp_sel_kernel, tq=tq, n_cmp=n_cmp, n_slc=n_slc, top_n=top_n),
        out_shape=(jax.ShapeDtypeStruct((b, s, NSA_HEADS * LANES), F32),
                   jax.ShapeDtypeStruct((b, g, s, LANES), BF16)),
        grid=(b, g, s // tq),
        in_specs=[pl.BlockSpec((1, tq, gw), lambda bi, gi, qi: (bi, qi, gi)),
                  pl.BlockSpec((1, 1, nb, LANES), lambda bi, gi, qi: (bi, gi, 0, 0)),
                  pl.BlockSpec((1, 1, nb, LANES), lambda bi, gi, qi: (bi, gi, 0, 0)),
                  pl.BlockSpec((LANES, LANES), lambda bi, gi, qi: (0, 0))],
        out_specs=(pl.BlockSpec((1, tq, gw), lambda bi, gi, qi: (bi, qi, gi)),
                   pl.BlockSpec((1, 1, tq, LANES), lambda bi, gi, qi: (bi, gi, qi, 0))),
        compiler_params=_params("parallel", "parallel", "parallel"),
        name="nsa_cmp_select",
    )(qr, k_cmp, v_cmp, overlap)


def _flash_init(m_sc, l_sc, acc_sc):
    m_sc[...] = jnp.full(m_sc.shape, NEG, F32)
    l_sc[...] = jnp.zeros(l_sc.shape, F32)
    acc_sc[...] = jnp.zeros(acc_sc.shape, F32)


def _dot_tn(a, b):
    return lax.dot_general(a.astype(BF16), b.astype(BF16), (((0,), (0,)), ((), ())),
                           preferred_element_type=F32)


def _flash_step(q, k, v, m_sc, l_sc, acc_sc, mask=None):
    s = _dot_nt(k, q)
    if mask is not None:
        s = jnp.where(mask, s, NEG)
    m_prev = m_sc[...]
    m_new = jnp.maximum(m_prev, jnp.max(s, axis=0, keepdims=True))
    alpha = jnp.exp(m_prev - m_new)
    p = jnp.exp(s - m_new)
    l_sc[...] = alpha * l_sc[...] + jnp.sum(p, axis=0, keepdims=True)
    acc_sc[...] = alpha * acc_sc[...] + _dot_tn(v, p)
    m_sc[...] = m_new


def _nsa_attn_kernel(q_ref, ks_ref, vs_ref, kw_ref, vw_ref, bias_ref, et_ref, ocmp_ref, gate_ref, o_ref,
                     kaug_sc, qaug_sc, m_sc, l_sc, acc_sc, *, tq, tk):
    gi = pl.program_id(1)
    qi = pl.program_id(2)
    dh = LANES
    s_len = kaug_sc.shape[0]
    q0 = qi * tq

    @pl.when(qi == 0)
    def _():
        kaug_sc[:, :dh] = ks_ref[0, 0]
        kaug_sc[:, dh:] = et_ref[...]

    bias = bias_ref[0, 0]
    for r in range(NSA_GROUP):
        qaug_sc[r * tq:(r + 1) * tq, :dh] = q_ref[0, :, r * dh:(r + 1) * dh]
        qaug_sc[r * tq:(r + 1) * tq, dh:] = bias

    _flash_init(m_sc, l_sc, acc_sc)
    n_k = (q0 + tq + tk - 1) // tk

    def slc_full(j, carry):
        k0 = pl.multiple_of(j * tk, tk)
        _flash_step(qaug_sc[...], kaug_sc[pl.ds(k0, tk), :], vs_ref[0, 0, pl.ds(k0, tk), :], m_sc, l_sc, acc_sc)
        return carry

    lax.fori_loop(0, n_k - 1, slc_full, 0)
    n_q = NSA_GROUP * tq
    k0 = pl.multiple_of((n_k - 1) * tk, tk)
    qpos = q0 + (lax.broadcasted_iota(jnp.int32, (tk, n_q), 1) & (tq - 1))
    causal = k0 + lax.broadcasted_iota(jnp.int32, (tk, n_q), 0) <= qpos
    _flash_step(qaug_sc[...], kaug_sc[pl.ds(k0, tk), :], vs_ref[0, 0, pl.ds(k0, tk), :], m_sc, l_sc, acc_sc,
                mask=causal)
    o_slc = acc_sc[...] / l_sc[...]

    band = WINDOW + tq
    start = pl.multiple_of(jnp.clip(q0 - WINDOW, 0, s_len - band), tq)
    diff = ((q0 - start) + (lax.broadcasted_iota(jnp.int32, (band, n_q), 1) & (tq - 1))
            - lax.broadcasted_iota(jnp.int32, (band, n_q), 0))
    s = jnp.where((diff >= 0) & (diff < WINDOW), _dot_nt(kw_ref[0, 0, pl.ds(start, band), :], qaug_sc[:, :dh]), NEG)
    p = jnp.exp(s - jnp.max(s, axis=0, keepdims=True))
    o_win = _dot_tn(vw_ref[0, 0, pl.ds(start, band), :], p) / jnp.sum(p, axis=0, keepdims=True)

    gates = jax.nn.sigmoid(gate_ref[0])
    gates_t = gates.T
    n_gate = 3 * NSA_GROUP

    def gate(r, branch, transposed):
        out = None
        for g in range(NSA_KV_HEADS):
            c = g * n_gate + r * 3 + branch
            v = gates_t[c:c + 1, :] if transposed else gates[:, c:c + 1]
            out = v if out is None else jnp.where(gi == g, v, out)
        return out

    for r in range(NSA_GROUP):
        cols = slice(r * tq, (r + 1) * tq)
        mixed_t = gate(r, 1, True) * o_slc[:, cols] + gate(r, 2, True) * o_win[:, cols]
        o_ref[0, :, r * dh:(r + 1) * dh] = (gate(r, 0, False) * ocmp_ref[0, :, r * dh:(r + 1) * dh]
                                            + mixed_t.T).astype(o_ref.dtype)


def _nsa_attn(qr, ks, vs, kw, vw, bias, e_t, o_cmp, proj, gate_col_block, *, tq, tk):
    b, s, _ = qr.shape
    g = NSA_KV_HEADS
    gw = NSA_GROUP * LANES
    assert tk % tq == 0 and s % tk == 0 and s >= WINDOW + tq and WINDOW % tq == 0 and tq & (tq - 1) == 0
    kv_spec = pl.BlockSpec((1, 1, s, LANES), lambda bi, gi, qi: (bi, gi, 0, 0))
    q_spec = pl.BlockSpec((1, tq, gw), lambda bi, gi, qi: (bi, qi, gi))
    rows = NSA_GROUP * tq
    return pl.pallas_call(
        functools.partial(_nsa_attn_kernel, tq=tq, tk=tk),
        out_shape=jax.ShapeDtypeStruct((b, s, NSA_HEADS * LANES), BF16),
        grid=(b, g, s // tq),
        in_specs=[q_spec, kv_spec, kv_spec, kv_spec, kv_spec,
                  pl.BlockSpec((1, 1, tq, LANES), lambda bi, gi, qi: (bi, gi, qi, 0)),
                  pl.BlockSpec((s, LANES), lambda bi, gi, qi: (0, 0)),
                  q_spec,
                  pl.BlockSpec((1, tq, LANES), lambda bi, gi, qi: (bi, qi, gate_col_block))],
        out_specs=q_spec,
        scratch_shapes=[pltpu.VMEM((s, 2 * LANES), BF16), pltpu.VMEM((rows, 2 * LANES), BF16),
                        pltpu.VMEM((1, rows), F32), pltpu.VMEM((1, rows), F32), pltpu.VMEM((LANES, rows), F32)],
        compiler_params=_params("parallel", "parallel", "arbitrary"),
        name="nsa_slc_win",
    )(qr, ks, vs, kw, vw, bias, e_t, o_cmp, proj)


def _mla_attn_kernel(q_ref, kn_ref, v_ref, kr_ref, tk_ref, o_ref, kcat_sc, m_sc, l_sc, acc_sc, *, tq, tk):
    qi = pl.program_id(2)
    nope = MLA_NOPE_DIM

    @pl.when(qi == 0)
    def _():
        prod = kr_ref[0] * tk_ref[...]
        rope2 = prod + pltpu.roll(prod, MLA_ROPE_DIM, 1)
        kcat_sc[:, :nope] = kn_ref[0]
        kcat_sc[:, nope:] = rope2.astype(kcat_sc.dtype)

    q0 = qi * tq
    _flash_init(m_sc, l_sc, acc_sc)
    q = q_ref[0]
    n_k = (q0 + tq + tk - 1) // tk

    def full(j, carry):
        k0 = pl.multiple_of(j * tk, tk)
        _flash_step(q, kcat_sc[pl.ds(k0, tk), :], v_ref[0, pl.ds(k0, tk), :], m_sc, l_sc, acc_sc)
        return carry

    lax.fori_loop(0, n_k - 1, full, 0)
    k0 = pl.multiple_of((n_k - 1) * tk, tk)
    causal = (k0 + lax.broadcasted_iota(jnp.int32, (tk, tq), 0)
              <= q0 + lax.broadcasted_iota(jnp.int32, (tk, tq), 1))
    _flash_step(q, kcat_sc[pl.ds(k0, tk), :], v_ref[0, pl.ds(k0, tk), :], m_sc, l_sc, acc_sc, mask=causal)
    o_ref[0] = (acc_sc[...] / l_sc[...]).T.astype(o_ref.dtype)


def _mla_attn(q2, kv_up, proj, kr_col_block, tab_k, *, tq, tk):
    b, s, _ = q2.shape
    hq = MLA_NOPE_DIM + 2 * MLA_ROPE_DIM
    assert tk % tq == 0 and s % tk == 0
    return pl.pallas_call(
        functools.partial(_mla_attn_kernel, tq=tq, tk=tk),
        out_shape=jax.ShapeDtypeStruct((b, s, MLA_HEADS * MLA_V_DIM), BF16),
        grid=(b, MLA_HEADS, s // tq),
        in_specs=[pl.BlockSpec((1, tq, hq), lambda bi, hi, qi: (bi, qi, hi)),
                  pl.BlockSpec((1, s, MLA_NOPE_DIM), lambda bi, hi, qi: (bi, 0, 2 * hi)),
                  pl.BlockSpec((1, s, MLA_V_DIM), lambda bi, hi, qi: (bi, 0, 2 * hi + 1)),
                  pl.BlockSpec((1, s, LANES), lambda bi, hi, qi: (bi, 0, kr_col_block)),
                  pl.BlockSpec((s, LANES), lambda bi, hi, qi: (0, 0))],
        out_specs=pl.BlockSpec((1, tq, MLA_V_DIM), lambda bi, hi, qi: (bi, qi, hi)),
        scratch_shapes=[pltpu.VMEM((s, hq), BF16), pltpu.VMEM((1, tq), F32), pltpu.VMEM((1, tq), F32),
                        pltpu.VMEM((MLA_V_DIM, tq), F32)],
        compiler_params=_params("parallel", "parallel", "arbitrary"),
        name="mla_attn",
    )(q2, kv_up, kv_up, proj, tab_k)


def _ffn_kernel(x_ref, g_ref, w1_ref, w3_ref, w2_ref, o_ref, hn_sc, acc_sc):
    f = pl.program_id(1)

    @pl.when(f == 0)
    def _():
        hn_sc[...] = _rms(x_ref[...], g_ref[...]).astype(hn_sc.dtype)
        acc_sc[...] = jnp.zeros(acc_sc.shape, F32)

    hn = hn_sc[...]
    z = jax.nn.silu(_dot(hn, w1_ref[...])) * _dot(hn, w3_ref[...])
    acc_sc[...] += _dot(z, w2_ref[...])

    @pl.when(f == pl.num_programs(1) - 1)
    def _():
        o_ref[...] = x_ref[...] + acc_sc[...]


def _ffn(x, gain, w1, w3, w2, *, tm, tf):
    t, d = x.shape
    ff = w1.shape[1]
    return pl.pallas_call(
        _ffn_kernel,
        out_shape=jax.ShapeDtypeStruct((t, d), F32),
        grid=(t // tm, ff // tf),
        in_specs=[pl.BlockSpec((tm, d), lambda i, f: (i, 0)),
                  pl.BlockSpec((1, d), lambda i, f: (0, 0)),
                  pl.BlockSpec((d, tf), lambda i, f: (0, f)),
                  pl.BlockSpec((d, tf), lambda i, f: (0, f)),
                  pl.BlockSpec((tf, d), lambda i, f: (f, 0))],
        out_specs=pl.BlockSpec((tm, d), lambda i, f: (i, 0)),
        scratch_shapes=[pltpu.VMEM((tm, d), BF16), pltpu.VMEM((tm, d), F32)],
        compiler_params=_params("parallel", "arbitrary"),
        name="dense_ffn",
    )(x, gain.reshape(1, d), w1, w3, w2)


def _row_copy(src_ref, dst_ref, sem, src_row, dst_row):
    return pltpu.make_async_copy(src_ref.at[pl.ds(src_row, 1)], dst_ref.at[pl.ds(dst_row, 1)], sem)


def _gather_rows(idx_ref, src_ref, dst_ref, sem, n_rows):
    def issue(r, carry):
        _row_copy(src_ref, dst_ref, sem, idx_ref[r], r).start()
        return carry

    def drain(r, carry):
        _row_copy(src_ref, dst_ref, sem, idx_ref[r], r).wait()
        return carry

    lax.fori_loop(0, n_rows, issue, 0, unroll=8)
    lax.fori_loop(0, n_rows, drain, 0, unroll=8)


def _row_gather_kernel(idx_ref, src_ref, out_ref, sem, *, chunk):
    _gather_rows(idx_ref, src_ref, out_ref, sem, chunk)


def _row_gather(src, idx, *, chunk, name):
    m = idx.shape[0]
    d = src.shape[1]
    return pl.pallas_call(
        functools.partial(_row_gather_kernel, chunk=chunk),
        out_shape=jax.ShapeDtypeStruct((m, d), src.dtype),
        grid=(m // chunk,),
        in_specs=[pl.BlockSpec((chunk,), lambda i: (i,), memory_space=pltpu.SMEM),
                  pl.BlockSpec(memory_space=pl.ANY)],
        out_specs=pl.BlockSpec((chunk, d), lambda i: (i, 0)),
        scratch_shapes=[pltpu.SemaphoreType.DMA(())],
        compiler_params=_params("arbitrary"),
        name=name,
    )(idx, src)


def _moe_ffn_kernel(te_ref, tl_ref, x_ref, g_ref, w1_ref, w3_ref, w2_ref, y_ref, hn_sc):
    i = pl.program_id(0)
    f = pl.program_id(1)
    live = tl_ref[i] > 0

    @pl.when(f == 0)
    def _():
        y_ref[...] = jnp.zeros(y_ref.shape, F32)

    @pl.when(live)
    def _():
        @pl.when(f == 0)
        def _():
            hn_sc[...] = _rms(x_ref[...], g_ref[...]).astype(hn_sc.dtype)

        hn = hn_sc[...]
        z = jax.nn.silu(_dot(hn, w1_ref[0])) * _dot(hn, w3_ref[0])
        y_ref[...] += _dot(z, w2_ref[0])


def _moe_ffn(x_sorted, gain, w1, w3, w2, tile_expert, tile_live, *, tm, tf):
    p, d = x_sorted.shape
    ff = w1.shape[2]

    def w_col(i, f, te, tl):
        return (te[i], 0, jnp.where(tl[i] > 0, f, 0))

    def w_row(i, f, te, tl):
        return (te[i], jnp.where(tl[i] > 0, f, 0), 0)

    return pl.pallas_call(
        _moe_ffn_kernel,
        out_shape=jax.ShapeDtypeStruct((p, d), F32),
        grid_spec=pltpu.PrefetchScalarGridSpec(
            num_scalar_prefetch=2,
            grid=(p // tm, ff // tf),
            in_specs=[pl.BlockSpec((tm, d), lambda i, f, te, tl: (i, 0)),
                      pl.BlockSpec((1, d), lambda i, f, te, tl: (0, 0)),
                      pl.BlockSpec((1, d, tf), w_col),
                      pl.BlockSpec((1, d, tf), w_col),
                      pl.BlockSpec((1, tf, d), w_row)],
            out_specs=pl.BlockSpec((tm, d), lambda i, f, te, tl: (i, 0)),
            scratch_shapes=[pltpu.VMEM((tm, d), BF16)]),
        compiler_params=_params("parallel", "arbitrary"),
        name="moe_ffn",
    )(tile_expert, tile_live, x_sorted, gain.reshape(1, d), w1, w3, w2)


def _moe_combine_kernel(*refs, final, tm):
    if final:
        slot_ref, x_ref, y_ref, r_ref, g_ref, o_ref, ybuf, sem = refs
    else:
        slot_ref, x_ref, y_ref, r_ref, o_ref, ybuf, sem = refs
    _gather_rows(slot_ref, y_ref, ybuf, sem, TOP_K * tm)
    route = r_ref[...]
    mix = None
    for k in range(TOP_K):
        term = route[:, TOP_K + k:TOP_K + k + 1] * ybuf[k * tm:(k + 1) * tm, :]
        mix = term if mix is None else mix + term
    out = x_ref[...] + mix
    if final:
        out = _rms(out, g_ref[...])
    o_ref[...] = out


def _moe_combine(x, y_sorted, slot_tiles, route, final_gain, *, tm):
    t, d = x.shape
    in_specs = [pl.BlockSpec((TOP_K * tm,), lambda i: (i,), memory_space=pltpu.SMEM),
                pl.BlockSpec((tm, d), lambda i: (i, 0)),
                pl.BlockSpec(memory_space=pl.ANY),
                pl.BlockSpec((tm, LANES), lambda i: (i, 0))]
    args = [slot_tiles, x, y_sorted, route]
    if final_gain is not None:
        in_specs.append(pl.BlockSpec((1, d), lambda i: (0, 0)))
        args.append(final_gain.reshape(1, d))
    return pl.pallas_call(
        functools.partial(_moe_combine_kernel, final=final_gain is not None, tm=tm),
        out_shape=jax.ShapeDtypeStruct((t, d), F32),
        grid=(t // tm,),
        in_specs=in_specs,
        out_specs=pl.BlockSpec((tm, d), lambda i: (i, 0)),
        scratch_shapes=[pltpu.VMEM((TOP_K * tm, d), F32), pltpu.SemaphoreType.DMA(())],
        compiler_params=_params("arbitrary"),
        name="moe_combine",
    )(*args)


def _router_kernel(x_ref, g_ref, w_ref, o_ref, *, n_exp):
    hn = _rms(x_ref[...], g_ref[...])
    w = w_ref[...]
    h_hi = hn.astype(BF16)
    h_lo = hn - h_hi.astype(F32)
    w_hi = w.astype(BF16)
    w_lo = w - w_hi.astype(F32)
    logits = _dot(h_hi, w_hi) + _dot(h_hi, w_lo) + _dot(h_lo, w_hi)
    lane = lax.broadcasted_iota(jnp.int32, logits.shape, 1)
    valid = lane < n_exp
    lg = jnp.where(valid, logits, NEG)
    ex = jnp.where(valid, jnp.exp(lg - jnp.max(lg, axis=-1, keepdims=True)), 0.0)
    probs = ex / jnp.sum(ex, axis=-1, keepdims=True)
    rest = jnp.where(valid, probs, -1.0)
    tops = []
    for _ in range(TOP_K):
        v = jnp.max(rest, axis=-1, keepdims=True)
        idx = jnp.min(jnp.where(rest == v, lane, LANES), axis=-1, keepdims=True)
        tops.append((v, idx))
        rest = jnp.where(lane == idx, -1.0, rest)
    den = tops[0][0]
    for v, _ in tops[1:]:
        den = den + v
    route = jnp.zeros(logits.shape, F32)
    for k, (v, idx) in enumerate(tops):
        route = jnp.where(lane == k, idx.astype(F32), route)
        route = jnp.where(lane == TOP_K + k, v / den, route)
    o_ref[...] = route


def _router(x, gain, router, *, tm):
    t, d = x.shape
    n_exp = router.shape[1]
    w = jnp.zeros((d, LANES), F32).at[:, :n_exp].set(router)
    return pl.pallas_call(
        functools.partial(_router_kernel, n_exp=n_exp),
        out_shape=jax.ShapeDtypeStruct((t, LANES), F32),
        grid=(t // tm,),
        in_specs=[pl.BlockSpec((tm, d), lambda i: (i, 0)),
                  pl.BlockSpec((1, d), lambda i: (0, 0)),
                  pl.BlockSpec((d, LANES), lambda i: (0, 0))],
        out_specs=pl.BlockSpec((tm, LANES), lambda i: (i, 0)),
        compiler_params=_params("parallel"),
        name="moe_router",
    )(x, gain.reshape(1, d), w)


def _final_norm_kernel(x_ref, g_ref, o_ref):
    o_ref[...] = _rms(x_ref[...], g_ref[...])


def _final_norm(x, gain, *, tm):
    t, d = x.shape
    return pl.pallas_call(
        _final_norm_kernel,
        out_shape=jax.ShapeDtypeStruct((t, d), F32),
        grid=(t // tm,),
        in_specs=[pl.BlockSpec((tm, d), lambda i: (i, 0)), pl.BlockSpec((1, d), lambda i: (0, 0))],
        out_specs=pl.BlockSpec((tm, d), lambda i: (i, 0)),
        compiler_params=_params("parallel"),
        name="final_norm",
    )(x, gain.reshape(1, d))


def _rope_tables(seq, dim):
    inv = 1.0 / (ROPE_THETA ** (jnp.arange(0, dim, 2, dtype=F32) / dim))
    ang = jnp.arange(seq, dtype=F32)[:, None] * inv[None, :]
    return jnp.cos(ang), jnp.sin(ang)


def _rot_half_cols(w):
    half = w.shape[-1] // 2
    return jnp.concatenate([w[..., half:], w[..., :half]], axis=-1)


def _tile(n, pref):
    for t in pref:
        if n % t == 0:
            return t
    return n


def kernel(x, attn_norm, w_in, w_out, cmp_k_pos, cmp_k_w1, cmp_k_w2, cmp_v_pos, cmp_v_w1, cmp_v_w2, mla_q_norm, mla_w_uq, mla_kv_norm, mla_w_ukv, ffn_norm, dense_w1, dense_w3, dense_w2, router, moe_w1, moe_w3, moe_w2, final_norm):
    b, s, d = x.shape
    depth = w_in.shape[0]
    t = b * s
    dh = LANES
    q_rank = mla_w_uq.shape[1]
    kv_rank = mla_w_ukv.shape[1]
    nsa_q = NSA_HEADS * dh
    nsa_kv = 6 * NSA_KV_HEADS * dh
    n_gate = 3 * NSA_HEADS
    o_q, o_kv, o_gate = 0, nsa_q, nsa_q + nsa_kv
    o_cq = o_gate + n_gate
    o_ckv = o_cq + q_rank
    o_kr = o_ckv + kv_rank
    assert o_kr + MLA_ROPE_DIM == w_in.shape[2]
    assert s % SLC_LEN == 0 and s % CMP_STRIDE == 0 and s // SLC_LEN <= LANES and s // CMP_STRIDE <= LANES
    c_cq = nsa_q + nsa_kv
    c_ckv = c_cq + q_rank
    c_kr = c_ckv + kv_rank
    c_gate = c_kr + 2 * MLA_ROPE_DIM
    n_in = c_gate + LANES
    assert q_rank % LANES == 0 and kv_rank % LANES == 0 and c_cq % q_rank == 0 and c_ckv % kv_rank == 0

    n_cmp = (s - CMP_LEN) // CMP_STRIDE + 1
    n_slc = s // SLC_LEN
    top_n = min(SLC_TOP, n_slc)

    cos_n, sin_n = _rope_tables(s, dh)
    cos_t = jnp.concatenate([cos_n, cos_n], axis=-1)
    sin_t = jnp.concatenate([-sin_n, sin_n], axis=-1)
    cos_m, sin_m = _rope_tables(s, MLA_ROPE_DIM)
    rot_cos = jnp.concatenate([cos_m, cos_m], axis=-1)
    rot_sin = jnp.concatenate([-sin_m, sin_m], axis=-1)
    mla_scale = (MLA_NOPE_DIM + MLA_ROPE_DIM) ** -0.5
    tab_q = jnp.concatenate([jnp.ones((s, MLA_NOPE_DIM), F32), rot_cos, rot_sin], axis=-1) * mla_scale
    tab_k = jnp.concatenate([rot_cos, rot_sin], axis=-1)

    cmp_tok = jnp.arange(LANES)[:, None] * CMP_STRIDE + jnp.arange(CMP_LEN)[None, :]
    overlap = jnp.mean(((cmp_tok[:, :, None] // SLC_LEN) == jnp.arange(LANES)[None, None, :]).astype(F32), axis=1)
    overlap = jnp.where(jnp.arange(LANES)[None, :] < n_slc, overlap, 0.0).astype(BF16)
    e_t = ((jnp.arange(s)[:, None] // SLC_LEN) == jnp.arange(LANES)[None, :]).astype(BF16)

    tm = _tile(t, (512, 256, 128))
    tq = _tile(s, (256, 128))
    tk_att = _tile(s, (512, 256, 128))
    hq = MLA_NOPE_DIM + 2 * MLA_ROPE_DIM

    xf = x.reshape(t, d)
    for i in range(depth):
        wi = w_in[i]
        kr_w = wi[:, o_kr:o_kr + MLA_ROPE_DIM]
        w_in2 = jnp.concatenate(
            [wi[:, o_q:o_q + nsa_q], wi[:, o_kv:o_kv + nsa_kv], wi[:, o_cq:o_cq + q_rank],
             wi[:, o_ckv:o_ckv + kv_rank], kr_w, _rot_half_cols(kr_w), wi[:, o_gate:o_gate + n_gate],
             jnp.zeros((d, LANES - n_gate), F32)], axis=1).astype(BF16)
        proj = _mm([xf], [w_in2], gain=attn_norm[i], tm=_tile(t, (1024, 512, 256, 128)),
                   tn=_tile(n_in, (512, 256, 128)), out_dtype=F32,
                   name="in_proj")
        proj3 = proj.reshape(b, s, n_in)

        qr, kc, vc, ks, vs, kw, vw = _nsa_prep(proj3, cos_t, sin_t, ts=_tile(s, (512, 256, 128)))
        k_cmp = _compress(kc, cmp_k_pos[i], cmp_k_w1[i], cmp_k_w2[i])
        v_cmp = _compress(vc, cmp_v_pos[i], cmp_v_w1[i], cmp_v_w2[i])
        o_cmp, sel = _cmp_sel(qr, k_cmp, v_cmp, overlap, tq=tq, n_cmp=n_cmp, n_slc=n_slc, top_n=top_n)
        o_nsa = _nsa_attn(qr, ks, vs, kw, vw, sel, e_t, o_cmp, proj3, c_gate // LANES, tq=tq, tk=tk_att)

        wq = mla_w_uq[i].reshape(q_rank, MLA_HEADS, MLA_NOPE_DIM + MLA_ROPE_DIM)
        wq_rope = wq[..., MLA_NOPE_DIM:]
        wq2 = jnp.concatenate([wq, _rot_half_cols(wq_rope)], axis=-1).reshape(q_rank, MLA_HEADS * hq).astype(BF16)
        q2 = _mm([proj], [wq2], x_cols=[c_cq // q_rank], gain=mla_q_norm[i], tab=tab_q, tm=tm, tn=hq,
                 out_dtype=BF16, name="mla_q_up")
        kv_up = _mm([proj], [mla_w_ukv[i].astype(BF16)], x_cols=[c_ckv // kv_rank], gain=mla_kv_norm[i],
                    tm=tm, tn=_tile(mla_w_ukv.shape[2], (512, 256, 128)), out_dtype=BF16, name="mla_kv_up")
        o_mla = _mla_attn(q2.reshape(b, s, -1), kv_up.reshape(b, s, -1), proj3, c_kr // LANES, tab_k, tq=tk_att,
                          tk=tk_att)

        wo = w_out[i].astype(BF16)
        xf = _mm([o_nsa.reshape(t, -1), o_mla.reshape(t, -1)], [wo[:nsa_q], wo[nsa_q:]], res=xf, tm=tm,
                 tn=_tile(d, (512, 256, 128)), out_dtype=F32, name="out_proj")

        j = i // 2
        if i % 2 == 0:
            xf = _ffn(xf, ffn_norm[i], dense_w1[j].astype(BF16), dense_w3[j].astype(BF16),
                      dense_w2[j].astype(BF16), tm=tm, tf=_tile(dense_w1.shape[2], (512, 256, 128)))
        else:
            route = _router(xf, ffn_norm[i], router[j], tm=tm)
            n_exp = router.shape[2]
            pair_expert = route[:, :TOP_K].astype(jnp.int32).reshape(-1)
            onehot = (pair_expert[:, None] == jnp.arange(n_exp)[None, :]).astype(jnp.int32)
            csum = jnp.cumsum(onehot, axis=0)
            rank = jnp.sum(onehot * (csum - 1), axis=1)
            group_end = jnp.cumsum((csum[-1] + tm - 1) // tm * tm)
            group_start = group_end - (csum[-1] + tm - 1) // tm * tm
            slot = (group_start[pair_expert] + rank).astype(jnp.int32)
            n_tiles = TOP_K * t // tm + n_exp
            tile_start = jnp.arange(n_tiles) * tm
            tile_expert = jnp.minimum(jnp.sum(tile_start[:, None] >= group_end[None, :], axis=1),
                                      n_exp - 1).astype(jnp.int32)
            tile_live = (tile_start < group_end[-1]).astype(jnp.int32)
            pair_token = jnp.arange(TOP_K * t, dtype=jnp.int32) // TOP_K
            slot_token = jnp.zeros((n_tiles * tm,), jnp.int32).at[slot].set(pair_token)
            x_sorted = _row_gather(xf, slot_token, chunk=tm, name="moe_gather")
            y_sorted = _moe_ffn(x_sorted, ffn_norm[i], moe_w1[j].astype(BF16), moe_w3[j].astype(BF16),
                                moe_w2[j].astype(BF16), tile_expert, tile_live, tm=tm,
                                tf=_tile(moe_w1.shape[3], (512, 256, 128)))
            slot_tiles = slot.reshape(t // tm, tm, TOP_K).transpose(0, 2, 1).reshape(-1)
            last = i == depth - 1
            xf = _moe_combine(xf, y_sorted, slot_tiles, route, final_norm if last else None, tm=tm)
            if last:
                return xf.reshape(b, s, d)
    return _final_norm(xf, final_norm, tm=tm).reshape(b, s, d)
```

```python
import functools

import jax
import jax.numpy as jnp
from jax import lax
from jax.experimental import pallas as pl
from jax.experimental.pallas import tpu as pltpu

F32 = jnp.float32
BF16 = jnp.bfloat16

NSA_HEADS = 8
NSA_KV_HEADS = 2
NSA_GROUP = NSA_HEADS // NSA_KV_HEADS
CMP_LEN = 32
CMP_STRIDE = 16
SLC_LEN = 64
SLC_SHIFT = 6
SLC_TOP = 16
N_FORCED_LOCAL = 2
FORCE_SCORE = 1.0e4
WINDOW = 512
MLA_HEADS = 8
MLA_NOPE_DIM = 128
MLA_ROPE_DIM = 64
MLA_V_DIM = 128
ROPE_THETA = 10000.0
RMS_EPS = 1e-6
TOP_K = 2

LANES = 128
NEG = -1.0e30
VMEM_LIMIT = 56 * 1024 * 1024


def _params(*sem):
    return pltpu.CompilerParams(dimension_semantics=sem, vmem_limit_bytes=VMEM_LIMIT)


def _rms(xf, g):
    return xf * lax.rsqrt(jnp.mean(xf * xf, axis=-1, keepdims=True) + RMS_EPS) * g


def _dot(a, b):
    return jnp.dot(a.astype(BF16), b.astype(BF16), preferred_element_type=F32)


def _dot_nt(a, b):
    return lax.dot_general(a.astype(BF16), b.astype(BF16), (((1,), (1,)), ((), ())),
                           preferred_element_type=F32)


def _mm_kernel(*refs, n_x, has_norm, has_tab, has_res):
    x_refs = refs[:n_x]
    w_refs = refs[n_x:2 * n_x]
    pos = 2 * n_x
    g_ref = tab_ref = res_ref = None
    if has_norm:
        g_ref = refs[pos]
        pos += 1
    if has_tab:
        tab_ref = refs[pos]
        pos += 1
    if has_res:
        res_ref = refs[pos]
        pos += 1
    o_ref = refs[pos]
    if has_norm:
        hn_sc = refs[pos + 1]

        @pl.when(pl.program_id(1) == 0)
        def _():
            hn_sc[...] = _rms(x_refs[0][...].astype(F32), g_ref[...]).astype(hn_sc.dtype)

    acc = None
    for x_ref, w_ref in zip(x_refs, w_refs):
        x = hn_sc[...] if has_norm else x_ref[...]
        d = _dot(x, w_ref[...])
        acc = d if acc is None else acc + d
    if has_tab:
        acc = acc * tab_ref[...]
    if has_res:
        acc = acc + res_ref[...]
    o_ref[...] = acc.astype(o_ref.dtype)


def _mm(xs, ws, *, x_cols=None, gain=None, tab=None, res=None, tm, tn, out_dtype, name):
    n_x = len(xs)
    assert gain is None or n_x == 1
    m = xs[0].shape[0]
    n = ws[0].shape[1]
    x_cols = x_cols or [0] * n_x
    in_specs, args = [], []
    for x, w, c in zip(xs, ws, x_cols):
        in_specs.append(pl.BlockSpec((tm, w.shape[0]), lambda i, j, c=c: (i, c)))
        args.append(x)
    for w in ws:
        in_specs.append(pl.BlockSpec((w.shape[0], tn), lambda i, j: (0, j)))
        args.append(w)
    if gain is not None:
        in_specs.append(pl.BlockSpec((1, gain.shape[-1]), lambda i, j: (0, 0)))
        args.append(gain.reshape(1, -1))
    if tab is not None:
        rb, cb = tab.shape[0] // tm, tab.shape[1] // tn
        in_specs.append(pl.BlockSpec((tm, tn), lambda i, j: (i % rb, j % cb)))
        args.append(tab)
    if res is not None:
        in_specs.append(pl.BlockSpec((tm, tn), lambda i, j: (i, j)))
        args.append(res)
    return pl.pallas_call(
        functools.partial(_mm_kernel, n_x=n_x, has_norm=gain is not None, has_tab=tab is not None,
                          has_res=res is not None),
        out_shape=jax.ShapeDtypeStruct((m, n), out_dtype),
        grid=(m // tm, n // tn),
        in_specs=in_specs,
        out_specs=pl.BlockSpec((tm, tn), lambda i, j: (i, j)),
        scratch_shapes=[pltpu.VMEM((tm, ws[0].shape[0]), BF16)] if gain is not None else [],
        compiler_params=_params("parallel", "arbitrary"),
        name=name,
    )(*args)


def _nsa_prep_kernel(p_ref, cos_ref, sin_ref, q_ref, kc_ref, vc_ref, ks_ref, vs_ref, kw_ref, vw_ref):
    cos = cos_ref[...]
    sin = sin_ref[...]
    dh = LANES

    def rope(x):
        return x * cos + pltpu.roll(x, dh // 2, 1) * sin

    scale = dh ** -0.5
    for h in range(NSA_HEADS):
        q_ref[0, :, h * dh:(h + 1) * dh] = (rope(p_ref[0, :, h * dh:(h + 1) * dh]) * scale).astype(q_ref.dtype)
    base = NSA_HEADS * dh
    outs = (kc_ref, vc_ref, ks_ref, vs_ref, kw_ref, vw_ref)
    for t, o_ref in enumerate(outs):
        for g in range(NSA_KV_HEADS):
            c0 = base + (t * NSA_KV_HEADS + g) * dh
            v = p_ref[0, :, c0:c0 + dh]
            if t % 2 == 0:
                v = rope(v)
            o_ref[0, g] = v.astype(o_ref.dtype)


def _nsa_prep(proj, cos_t, sin_t, *, ts):
    b, s, _ = proj.shape
    g, dh = NSA_KV_HEADS, LANES
    width = (NSA_HEADS + 6 * g) * dh
    kv_spec = pl.BlockSpec((1, g, ts, dh), lambda bi, si: (bi, 0, si, 0))
    kv = lambda dt: jax.ShapeDtypeStruct((b, g, s, dh), dt)
    return pl.pallas_call(
        _nsa_prep_kernel,
        out_shape=(jax.ShapeDtypeStruct((b, s, NSA_HEADS * dh), BF16),
                   kv(F32), kv(F32), kv(BF16), kv(BF16), kv(BF16), kv(BF16)),
        grid=(b, s // ts),
        in_specs=[pl.BlockSpec((1, ts, width), lambda bi, si: (bi, si, 0)),
                  pl.BlockSpec((ts, dh), lambda bi, si: (si, 0)),
                  pl.BlockSpec((ts, dh), lambda bi, si: (si, 0))],
        out_specs=(pl.BlockSpec((1, ts, NSA_HEADS * dh), lambda bi, si: (bi, si, 0)),
                   kv_spec, kv_spec, kv_spec, kv_spec, kv_spec, kv_spec),
        compiler_params=_params("parallel", "parallel"),
        name="nsa_prep",
    )(proj, cos_t, sin_t)


def _compress_kernel(t_ref, pa_ref, pb_ref, w1a_ref, w1b_ref, w2_ref, o_ref):
    t = t_ref[0, 0]
    ya = _dot(t + pa_ref[...], w1a_ref[...])
    yb = _dot(t + pb_ref[...], w1b_ref[...])
    n = yb.shape[0]
    pre = ya + pltpu.roll(yb, n - 1, 0)
    o_ref[0, 0] = _dot(jax.nn.gelu(pre), w2_ref[...])


def _compress(t, pos_emb, w1, w2):
    b, g, s, dh = t.shape
    nb = s // CMP_STRIDE
    half = CMP_STRIDE * dh
    hid = w1.shape[1]
    t16 = t.reshape(b, g, nb, half)
    pa = pos_emb[:CMP_STRIDE].reshape(1, half)
    pb = pos_emb[CMP_STRIDE:].reshape(1, half)
    const = lambda shape: pl.BlockSpec(shape, lambda bi, gi: (0, 0))
    return pl.pallas_call(
        _compress_kernel,
        out_shape=jax.ShapeDtypeStruct((b, g, nb, dh), F32),
        grid=(b, g),
        in_specs=[pl.BlockSpec((1, 1, nb, half), lambda bi, gi: (bi, gi, 0, 0)),
                  const((1, half)), const((1, half)),
                  const((half, hid)), const((half, hid)), const((hid, dh))],
        out_specs=pl.BlockSpec((1, 1, nb, dh), lambda bi, gi: (bi, gi, 0, 0)),
        compiler_params=_params("parallel", "parallel"),
        name="nsa_compress",
    )(t16, pa, pb, w1[:half].astype(BF16), w1[half:].astype(BF16), w2.astype(BF16))


def _cmp_sel_kernel(q_ref, kc_ref, vc_ref, ov_ref, ocmp_ref, sel_ref, *, tq, n_cmp, n_slc, top_n):
    qi = pl.program_id(2)
    dh = LANES
    n_c = kc_ref.shape[2]
    n_j = ov_ref.shape[1]
    pos = qi * tq + lax.broadcasted_iota(jnp.int32, (n_c, tq), 1)
    c = lax.broadcasted_iota(jnp.int32, (n_c, tq), 0)
    valid = (c * CMP_STRIDE + (CMP_LEN - 1) <= pos) & (c < n_cmp)
    kc = kc_ref[0, 0]
    vc = vc_ref[0, 0]
    psum = jnp.zeros((n_c, tq), F32)
    for r in range(NSA_GROUP):
        s = jnp.where(valid, _dot_nt(kc, q_ref[0, :, r * dh:(r + 1) * dh]), NEG)
        m = jnp.max(s, axis=0, keepdims=True)
        p = jnp.where(valid, jnp.exp(s - m), 0.0)
        p = p / jnp.maximum(jnp.sum(p, axis=0, keepdims=True), 1e-30)
        ocmp_ref[0, 0, r * dh:(r + 1) * dh, :] = _dot_tn(vc, p)
        psum = psum + p
    hi = psum.astype(BF16)
    lo = psum - hi.astype(F32)
    imp = _dot_tn(ov_ref[...], hi) + _dot_tn(ov_ref[...], lo)
    pos = qi * tq + lax.broadcasted_iota(jnp.int32, (n_j, tq), 1)
    j = lax.broadcasted_iota(jnp.int32, (n_j, tq), 0)
    causal_blk = j * SLC_LEN <= pos
    dist = lax.shift_right_logical(pos, SLC_SHIFT) - j
    forced = (j == 0) | ((dist >= 0) & (dist < N_FORCED_LOCAL))
    score = jnp.where(causal_blk, jnp.where(forced, FORCE_SCORE, imp), -jnp.inf)
    rank = jnp.zeros((n_j, tq), jnp.int32)
    for i in range(n_slc):
        row = score[i:i + 1, :]
        beats = (row > score) | ((row == score) & (j > i))
        rank = rank + jnp.where(beats, 1, 0)
    bias = jnp.where((rank < top_n) | (j >= n_slc), 0.0, NEG)
    if n_j < LANES:
        bias = jnp.concatenate([bias, jnp.zeros((LANES - n_j, tq), F32)], axis=0)
    sel_ref[0, 0] = bias.T.astype(sel_ref.dtype)


def _cmp_sel(qr, k_cmp, v_cmp, overlap, *, tq, n_cmp, n_slc, top_n):
    b, s, _ = qr.shape
    g = NSA_KV_HEADS
    gw = NSA_GROUP * LANES
    nb = k_cmp.shape[2]
    return pl.pallas_call(
        functools.partial(_cmp_sel_kernel, tq=tq, n_cmp=n_cmp, n_slc=n_slc, top_n=top_n),
        out_shape=(jax.ShapeDtypeStruct((b, g, gw, s), F32),
                   jax.ShapeDtypeStruct((b, g, s, LANES), BF16)),
        grid=(b, g, s // tq),
        in_specs=[pl.BlockSpec((1, tq, gw), lambda bi, gi, qi: (bi, qi, gi)),
                  pl.BlockSpec((1, 1, nb, LANES), lambda bi, gi, qi: (bi, gi, 0, 0)),
                  pl.BlockSpec((1, 1, nb, LANES), lambda bi, gi, qi: (bi, gi, 0, 0)),
                  pl.BlockSpec(overlap.shape, lambda bi, gi, qi: (0, 0))],
        out_specs=(pl.BlockSpec((1, 1, gw, tq), lambda bi, gi, qi: (bi, gi, 0, qi)),
                   pl.BlockSpec((1, 1, tq, LANES), lambda bi, gi, qi: (bi, gi, qi, 0))),
        compiler_params=_params("parallel", "parallel", "parallel"),
        name="nsa_cmp_select",
    )(qr, k_cmp, v_cmp, overlap)


def _flash_init(m_sc, l_sc, acc_sc):
    m_sc[...] = jnp.full(m_sc.shape, NEG, F32)
    l_sc[...] = jnp.zeros(l_sc.shape, F32)
    acc_sc[...] = jnp.zeros(acc_sc.shape, F32)


def _dot_tn(a, b):
    return lax.dot_general(a.astype(BF16), b.astype(BF16), (((0,), (0,)), ((), ())),
                           preferred_element_type=F32)


def _flash_step(q, k, v, m_sc, l_sc, acc_sc, mask=None):
    s = _dot_nt(k, q)
    if mask is not None:
        s = jnp.where(mask, s, NEG)
    m_prev = m_sc[...]
    m_new = jnp.maximum(m_prev, jnp.max(s, axis=0, keepdims=True))
    alpha = jnp.exp(m_prev - m_new)
    p = jnp.exp(s - m_new)
    l_sc[...] = alpha * l_sc[...] + jnp.sum(p, axis=0, keepdims=True)
    acc_sc[...] = alpha * acc_sc[...] + _dot_tn(v, p)
    m_sc[...] = m_new


def _nsa_attn_kernel(q_ref, ks_ref, vs_ref, kw_ref, vw_ref, bias_ref, et_ref, ocmp_ref, gate_ref, o_ref,
                     kaug_sc, qaug_sc, m_sc, l_sc, acc_sc, *, tq, tk):
    gi = pl.program_id(1)
    qi = pl.program_id(2)
    dh = LANES
    s_len = kaug_sc.shape[0]
    q0 = qi * tq

    @pl.when(qi == 0)
    def _():
        kaug_sc[:, :dh] = ks_ref[0, 0]
        kaug_sc[:, dh:] = et_ref[...]

    bias = bias_ref[0, 0]
    for r in range(NSA_GROUP):
        qaug_sc[r * tq:(r + 1) * tq, :dh] = q_ref[0, :, r * dh:(r + 1) * dh]
        qaug_sc[r * tq:(r + 1) * tq, dh:] = bias

    _flash_init(m_sc, l_sc, acc_sc)
    n_k = (q0 + tq + tk - 1) // tk

    def slc_full(j, carry):
        k0 = pl.multiple_of(j * tk, tk)
        _flash_step(qaug_sc[...], kaug_sc[pl.ds(k0, tk), :], vs_ref[0, 0, pl.ds(k0, tk), :], m_sc, l_sc, acc_sc)
        return carry

    lax.fori_loop(0, n_k - 1, slc_full, 0)
    n_q = NSA_GROUP * tq
    k0 = pl.multiple_of((n_k - 1) * tk, tk)
    qpos = q0 + (lax.broadcasted_iota(jnp.int32, (tk, n_q), 1) & (tq - 1))
    causal = k0 + lax.broadcasted_iota(jnp.int32, (tk, n_q), 0) <= qpos
    _flash_step(qaug_sc[...], kaug_sc[pl.ds(k0, tk), :], vs_ref[0, 0, pl.ds(k0, tk), :], m_sc, l_sc, acc_sc,
                mask=causal)
    o_slc = acc_sc[...] / l_sc[...]

    band = WINDOW + tq
    start = pl.multiple_of(jnp.clip(q0 - WINDOW, 0, s_len - band), tq)
    diff = ((q0 - start) + (lax.broadcasted_iota(jnp.int32, (band, n_q), 1) & (tq - 1))
            - lax.broadcasted_iota(jnp.int32, (band, n_q), 0))
    s = jnp.where((diff >= 0) & (diff < WINDOW), _dot_nt(kw_ref[0, 0, pl.ds(start, band), :], qaug_sc[:, :dh]), NEG)
    p = jnp.exp(s - jnp.max(s, axis=0, keepdims=True))
    o_win = _dot_tn(vw_ref[0, 0, pl.ds(start, band), :], p) / jnp.sum(p, axis=0, keepdims=True)

    gates_t = jax.nn.sigmoid(gate_ref[0]).T
    n_gate = 3 * NSA_GROUP

    def gate(r, branch):
        out = None
        for g in range(NSA_KV_HEADS):
            c = g * n_gate + r * 3 + branch
            v = gates_t[c:c + 1, :]
            out = v if out is None else jnp.where(gi == g, v, out)
        return out

    for r in range(NSA_GROUP):
        cols = slice(r * tq, (r + 1) * tq)
        mixed_t = (gate(r, 0) * ocmp_ref[0, 0, r * dh:(r + 1) * dh, :] + gate(r, 1) * o_slc[:, cols]
                   + gate(r, 2) * o_win[:, cols])
        o_ref[0, :, r * dh:(r + 1) * dh] = mixed_t.T.astype(o_ref.dtype)


def _nsa_attn(qr, ks, vs, kw, vw, bias, e_t, o_cmp, proj, gate_col_block, *, tq, tk):
    b, s, _ = qr.shape
    g = NSA_KV_HEADS
    gw = NSA_GROUP * LANES
    assert tk % tq == 0 and s % tk == 0 and s >= WINDOW + tq and WINDOW % tq == 0 and tq & (tq - 1) == 0
    kv_spec = pl.BlockSpec((1, 1, s, LANES), lambda bi, gi, qi: (bi, gi, 0, 0))
    q_spec = pl.BlockSpec((1, tq, gw), lambda bi, gi, qi: (bi, qi, gi))
    rows = NSA_GROUP * tq
    return pl.pallas_call(
        functools.partial(_nsa_attn_kernel, tq=tq, tk=tk),
        out_shape=jax.ShapeDtypeStruct((b, s, NSA_HEADS * LANES), BF16),
        grid=(b, g, s // tq),
        in_specs=[q_spec, kv_spec, kv_spec, kv_spec, kv_spec,
                  pl.BlockSpec((1, 1, tq, LANES), lambda bi, gi, qi: (bi, gi, qi, 0)),
                  pl.BlockSpec((s, LANES), lambda bi, gi, qi: (0, 0)),
                  pl.BlockSpec((1, 1, gw, tq), lambda bi, gi, qi: (bi, gi, 0, qi)),
                  pl.BlockSpec((1, tq, LANES), lambda bi, gi, qi: (bi, qi, gate_col_block))],
        out_specs=q_spec,
        scratch_shapes=[pltpu.VMEM((s, 2 * LANES), BF16), pltpu.VMEM((rows, 2 * LANES), BF16),
                        pltpu.VMEM((1, rows), F32), pltpu.VMEM((1, rows), F32), pltpu.VMEM((LANES, rows), F32)],
        compiler_params=_params("parallel", "parallel", "arbitrary"),
        name="nsa_slc_win",
    )(qr, ks, vs, kw, vw, bias, e_t, o_cmp, proj)


def _mla_attn_kernel(q_ref, kv_ref, kr_ref, tk_ref, o_ref, kcat_sc, m_sc, l_sc, acc_sc, *, tq, tk, heads):
    qi = pl.program_id(2)
    nope, dv = MLA_NOPE_DIM, MLA_V_DIM
    hq = kcat_sc.shape[2]

    @pl.when(qi == 0)
    def _():
        prod = kr_ref[0] * tk_ref[...]
        rope2 = prod + pltpu.roll(prod, MLA_ROPE_DIM, 1)
        for h in range(heads):
            kcat_sc[h, :, :nope] = kv_ref[0, :, h * (nope + dv):h * (nope + dv) + nope]
            kcat_sc[h, :, nope:] = rope2.astype(kcat_sc.dtype)

    q0 = qi * tq
    _flash_init(m_sc, l_sc, acc_sc)
    n_k = (q0 + tq + tk - 1) // tk

    def step(k0, mask):
        for h in range(heads):
            v0 = h * (nope + dv) + nope
            _flash_step(q_ref[0, :, h * hq:(h + 1) * hq], kcat_sc[h, pl.ds(k0, tk), :],
                        kv_ref[0, pl.ds(k0, tk), v0:v0 + dv], m_sc.at[h], l_sc.at[h], acc_sc.at[h], mask=mask)

    def full(j, carry):
        step(pl.multiple_of(j * tk, tk), None)
        return carry

    lax.fori_loop(0, n_k - 1, full, 0)
    k0 = pl.multiple_of((n_k - 1) * tk, tk)
    causal = (k0 + lax.broadcasted_iota(jnp.int32, (tk, tq), 0)
              <= q0 + lax.broadcasted_iota(jnp.int32, (tk, tq), 1))
    step(k0, causal)
    for h in range(heads):
        o_ref[0, :, h * dv:(h + 1) * dv] = (acc_sc[h] / l_sc[h]).T.astype(o_ref.dtype)


def _mla_attn(q2, kv_up, proj, kr_col_block, tab_k, *, tq, tk, heads):
    b, s, _ = q2.shape
    hq = MLA_NOPE_DIM + 2 * MLA_ROPE_DIM
    hkv = MLA_NOPE_DIM + MLA_V_DIM
    assert tk % tq == 0 and s % tk == 0 and MLA_HEADS % heads == 0
    return pl.pallas_call(
        functools.partial(_mla_attn_kernel, tq=tq, tk=tk, heads=heads),
        out_shape=jax.ShapeDtypeStruct((b, s, MLA_HEADS * MLA_V_DIM), BF16),
        grid=(b, MLA_HEADS // heads, s // tq),
        in_specs=[pl.BlockSpec((1, tq, heads * hq), lambda bi, hi, qi: (bi, qi, hi)),
                  pl.BlockSpec((1, s, heads * hkv), lambda bi, hi, qi: (bi, 0, hi)),
                  pl.BlockSpec((1, s, LANES), lambda bi, hi, qi: (bi, 0, kr_col_block)),
                  pl.BlockSpec((s, LANES), lambda bi, hi, qi: (0, 0))],
        out_specs=pl.BlockSpec((1, tq, heads * MLA_V_DIM), lambda bi, hi, qi: (bi, qi, hi)),
        scratch_shapes=[pltpu.VMEM((heads, s, hq), BF16), pltpu.VMEM((heads, 1, tq), F32),
                        pltpu.VMEM((heads, 1, tq), F32), pltpu.VMEM((heads, MLA_V_DIM, tq), F32)],
        compiler_params=_params("parallel", "parallel", "arbitrary"),
        name="mla_attn",
    )(q2, kv_up, proj, tab_k)


def _ffn_kernel(x_ref, g_ref, w1_ref, w3_ref, w2_ref, o_ref, hn_sc, acc_sc):
    f = pl.program_id(1)

    @pl.when(f == 0)
    def _():
        hn_sc[...] = _rms(x_ref[...], g_ref[...]).astype(hn_sc.dtype)
        acc_sc[...] = jnp.zeros(acc_sc.shape, F32)

    hn = hn_sc[...]
    z = jax.nn.silu(_dot(hn, w1_ref[...])) * _dot(hn, w3_ref[...])
    acc_sc[...] += _dot(z, w2_ref[...])

    @pl.when(f == pl.num_programs(1) - 1)
    def _():
        o_ref[...] = x_ref[...] + acc_sc[...]


def _ffn(x, gain, w1, w3, w2, *, tm, tf):
    t, d = x.shape
    ff = w1.shape[1]
    return pl.pallas_call(
        _ffn_kernel,
        out_shape=jax.ShapeDtypeStruct((t, d), F32),
        grid=(t // tm, ff // tf),
        in_specs=[pl.BlockSpec((tm, d), lambda i, f: (i, 0)),
                  pl.BlockSpec((1, d), lambda i, f: (0, 0)),
                  pl.BlockSpec((d, tf), lambda i, f: (0, f)),
                  pl.BlockSpec((d, tf), lambda i, f: (0, f)),
                  pl.BlockSpec((tf, d), lambda i, f: (f, 0))],
        out_specs=pl.BlockSpec((tm, d), lambda i, f: (i, 0)),
        scratch_shapes=[pltpu.VMEM((tm, d), BF16), pltpu.VMEM((tm, d), F32)],
        compiler_params=_params("parallel", "arbitrary"),
        name="dense_ffn",
    )(x, gain.reshape(1, d), w1, w3, w2)


def _row_copy(src_ref, dst_ref, sem, src_row, dst_row):
    return pltpu.make_async_copy(src_ref.at[pl.ds(src_row, 1)], dst_ref.at[pl.ds(dst_row, 1)], sem)


def _gather_rows(idx_ref, src_ref, dst_ref, sem, n_rows):
    def issue(r, carry):
        _row_copy(src_ref, dst_ref, sem, idx_ref[r], r).start()
        return carry

    def drain(r, carry):
        _row_copy(src_ref, dst_ref, sem, idx_ref[r], r).wait()
        return carry

    lax.fori_loop(0, n_rows, issue, 0, unroll=8)
    lax.fori_loop(0, n_rows, drain, 0, unroll=8)


def _row_gather_kernel(idx_ref, src_ref, out_ref, sem, *, chunk):
    _gather_rows(idx_ref, src_ref, out_ref, sem, chunk)


def _row_gather(src, idx, *, chunk, name):
    m = idx.shape[0]
    d = src.shape[1]
    return pl.pallas_call(
        functools.partial(_row_gather_kernel, chunk=chunk),
        out_shape=jax.ShapeDtypeStruct((m, d), src.dtype),
        grid=(m // chunk,),
        in_specs=[pl.BlockSpec((chunk,), lambda i: (i,), memory_space=pltpu.SMEM),
                  pl.BlockSpec(memory_space=pl.ANY)],
        out_specs=pl.BlockSpec((chunk, d), lambda i: (i, 0)),
        scratch_shapes=[pltpu.SemaphoreType.DMA(())],
        compiler_params=_params("arbitrary"),
        name=name,
    )(idx, src)


def _moe_ffn_kernel(te_ref, tl_ref, x_ref, g_ref, w1_ref, w3_ref, w2_ref, y_ref, hn_sc):
    i = pl.program_id(0)
    f = pl.program_id(1)
    live = tl_ref[i] > 0

    @pl.when(f == 0)
    def _():
        y_ref[...] = jnp.zeros(y_ref.shape, F32)

    @pl.when(live)
    def _():
        @pl.when(f == 0)
        def _():
            hn_sc[...] = _rms(x_ref[...], g_ref[...]).astype(hn_sc.dtype)

        hn = hn_sc[...]
        z = jax.nn.silu(_dot(hn, w1_ref[0])) * _dot(hn, w3_ref[0])
        y_ref[...] += _dot(z, w2_ref[0])


def _moe_ffn(x_sorted, gain, w1, w3, w2, tile_expert, tile_live, *, tm, tf):
    p, d = x_sorted.shape
    ff = w1.shape[2]

    def w_col(i, f, te, tl):
        return (te[i], 0, jnp.where(tl[i] > 0, f, 0))

    def w_row(i, f, te, tl):
        return (te[i], jnp.where(tl[i] > 0, f, 0), 0)

    return pl.pallas_call(
        _moe_ffn_kernel,
        out_shape=jax.ShapeDtypeStruct((p, d), F32),
        grid_spec=pltpu.PrefetchScalarGridSpec(
            num_scalar_prefetch=2,
            grid=(p // tm, ff // tf),
            in_specs=[pl.BlockSpec((tm, d), lambda i, f, te, tl: (i, 0)),
                      pl.BlockSpec((1, d), lambda i, f, te, tl: (0, 0)),
                      pl.BlockSpec((1, d, tf), w_col),
                      pl.BlockSpec((1, d, tf), w_col),
                      pl.BlockSpec((1, tf, d), w_row)],
            out_specs=pl.BlockSpec((tm, d), lambda i, f, te, tl: (i, 0)),
            scratch_shapes=[pltpu.VMEM((tm, d), BF16)]),
        compiler_params=_params("parallel", "arbitrary"),
        name="moe_ffn",
    )(tile_expert, tile_live, x_sorted, gain.reshape(1, d), w1, w3, w2)


def _moe_combine_kernel(*refs, final, tm):
    if final:
        slot_ref, x_ref, y_ref, r_ref, g_ref, o_ref, ybuf, sem = refs
    else:
        slot_ref, x_ref, y_ref, r_ref, o_ref, ybuf, sem = refs
    _gather_rows(slot_ref, y_ref, ybuf, sem, TOP_K * tm)
    route = r_ref[...]
    mix = None
    for k in range(TOP_K):
        term = route[:, TOP_K + k:TOP_K + k + 1] * ybuf[k * tm:(k + 1) * tm, :]
        mix = term if mix is None else mix + term
    out = x_ref[...] + mix
    if final:
        out = _rms(out, g_ref[...])
    o_ref[...] = out


def _moe_combine(x, y_sorted, slot_tiles, route, final_gain, *, tm):
    t, d = x.shape
    in_specs = [pl.BlockSpec((TOP_K * tm,), lambda i: (i,), memory_space=pltpu.SMEM),
                pl.BlockSpec((tm, d), lambda i: (i, 0)),
                pl.BlockSpec(memory_space=pl.ANY),
                pl.BlockSpec((tm, LANES), lambda i: (i, 0))]
    args = [slot_tiles, x, y_sorted, route]
    if final_gain is not None:
        in_specs.append(pl.BlockSpec((1, d), lambda i: (0, 0)))
        args.append(final_gain.reshape(1, d))
    return pl.pallas_call(
        functools.partial(_moe_combine_kernel, final=final_gain is not None, tm=tm),
        out_shape=jax.ShapeDtypeStruct((t, d), F32),
        grid=(t // tm,),
        in_specs=in_specs,
        out_specs=pl.BlockSpec((tm, d), lambda i: (i, 0)),
        scratch_shapes=[pltpu.VMEM((TOP_K * tm, d), F32), pltpu.SemaphoreType.DMA(())],
        compiler_params=_params("arbitrary"),
        name="moe_combine",
    )(*args)


def _router_kernel(x_ref, g_ref, w_ref, o_ref, *, n_exp):
    hn = _rms(x_ref[...], g_ref[...])
    w = w_ref[...]
    h_hi = hn.astype(BF16)
    h_lo = hn - h_hi.astype(F32)
    w_hi = w.astype(BF16)
    w_lo = w - w_hi.astype(F32)
    logits = _dot(h_hi, w_hi) + _dot(h_hi, w_lo) + _dot(h_lo, w_hi)
    lane = lax.broadcasted_iota(jnp.int32, logits.shape, 1)
    valid = lane < n_exp
    lg = jnp.where(valid, logits, NEG)
    ex = jnp.where(valid, jnp.exp(lg - jnp.max(lg, axis=-1, keepdims=True)), 0.0)
    probs = ex / jnp.sum(ex, axis=-1, keepdims=True)
    rest = jnp.where(valid, probs, -1.0)
    tops = []
    for _ in range(TOP_K):
        v = jnp.max(rest, axis=-1, keepdims=True)
        idx = jnp.min(jnp.where(rest == v, lane, LANES), axis=-1, keepdims=True)
        tops.append((v, idx))
        rest = jnp.where(lane == idx, -1.0, rest)
    den = tops[0][0]
    for v, _ in tops[1:]:
        den = den + v
    route = jnp.zeros(logits.shape, F32)
    for k, (v, idx) in enumerate(tops):
        route = jnp.where(lane == k, idx.astype(F32), route)
        route = jnp.where(lane == TOP_K + k, v / den, route)
    o_ref[...] = route


def _router(x, gain, router, *, tm):
    t, d = x.shape
    n_exp = router.shape[1]
    w = jnp.zeros((d, LANES), F32).at[:, :n_exp].set(router)
    return pl.pallas_call(
        functools.partial(_router_kernel, n_exp=n_exp),
        out_shape=jax.ShapeDtypeStruct((t, LANES), F32),
        grid=(t // tm,),
        in_specs=[pl.BlockSpec((tm, d), lambda i: (i, 0)),
                  pl.BlockSpec((1, d), lambda i: (0, 0)),
                  pl.BlockSpec((d, LANES), lambda i: (0, 0))],
        out_specs=pl.BlockSpec((tm, LANES), lambda i: (i, 0)),
        compiler_params=_params("parallel"),
        name="moe_router",
    )(x, gain.reshape(1, d), w)


def _final_norm_kernel(x_ref, g_ref, o_ref):
    o_ref[...] = _rms(x_ref[...], g_ref[...])


def _final_norm(x, gain, *, tm):
    t, d = x.shape
    return pl.pallas_call(
        _final_norm_kernel,
        out_shape=jax.ShapeDtypeStruct((t, d), F32),
        grid=(t // tm,),
        in_specs=[pl.BlockSpec((tm, d), lambda i: (i, 0)), pl.BlockSpec((1, d), lambda i: (0, 0))],
        out_specs=pl.BlockSpec((tm, d), lambda i: (i, 0)),
        compiler_params=_params("parallel"),
        name="final_norm",
    )(x, gain.reshape(1, d))


def _rope_tables(seq, dim):
    inv = 1.0 / (ROPE_THETA ** (jnp.arange(0, dim, 2, dtype=F32) / dim))
    ang = jnp.arange(seq, dtype=F32)[:, None] * inv[None, :]
    return jnp.cos(ang), jnp.sin(ang)


def _rot_half_cols(w):
    half = w.shape[-1] // 2
    return jnp.concatenate([w[..., half:], w[..., :half]], axis=-1)


def _tile(n, pref):
    for t in pref:
        if n % t == 0:
            return t
    return n


def kernel(x, attn_norm, w_in, w_out, cmp_k_pos, cmp_k_w1, cmp_k_w2, cmp_v_pos, cmp_v_w1, cmp_v_w2, mla_q_norm, mla_w_uq, mla_kv_norm, mla_w_ukv, ffn_norm, dense_w1, dense_w3, dense_w2, router, moe_w1, moe_w3, moe_w2, final_norm):
    b, s, d = x.shape
    depth = w_in.shape[0]
    t = b * s
    dh = LANES
    q_rank = mla_w_uq.shape[1]
    kv_rank = mla_w_ukv.shape[1]
    nsa_q = NSA_HEADS * dh
    nsa_kv = 6 * NSA_KV_HEADS * dh
    n_gate = 3 * NSA_HEADS
    o_q, o_kv, o_gate = 0, nsa_q, nsa_q + nsa_kv
    o_cq = o_gate + n_gate
    o_ckv = o_cq + q_rank
    o_kr = o_ckv + kv_rank
    assert o_kr + MLA_ROPE_DIM == w_in.shape[2]
    assert s % SLC_LEN == 0 and s % CMP_STRIDE == 0 and s // SLC_LEN <= LANES and s // CMP_STRIDE <= LANES
    c_cq = nsa_q + nsa_kv
    c_ckv = c_cq + q_rank
    c_kr = c_ckv + kv_rank
    c_gate = c_kr + 2 * MLA_ROPE_DIM
    n_in = c_gate + LANES
    assert q_rank % LANES == 0 and kv_rank % LANES == 0 and c_cq % q_rank == 0 and c_ckv % kv_rank == 0

    n_cmp = (s - CMP_LEN) // CMP_STRIDE + 1
    n_slc = s // SLC_LEN
    top_n = min(SLC_TOP, n_slc)

    cos_n, sin_n = _rope_tables(s, dh)
    cos_t = jnp.concatenate([cos_n, cos_n], axis=-1)
    sin_t = jnp.concatenate([-sin_n, sin_n], axis=-1)
    cos_m, sin_m = _rope_tables(s, MLA_ROPE_DIM)
    rot_cos = jnp.concatenate([cos_m, cos_m], axis=-1)
    rot_sin = jnp.concatenate([-sin_m, sin_m], axis=-1)
    mla_scale = (MLA_NOPE_DIM + MLA_ROPE_DIM) ** -0.5
    tab_q = jnp.concatenate([jnp.ones((s, MLA_NOPE_DIM), F32), rot_cos, rot_sin], axis=-1) * mla_scale
    tab_q = jnp.tile(tab_q, (1, MLA_HEADS))
    tab_k = jnp.concatenate([rot_cos, rot_sin], axis=-1)

    cmp_tok = jnp.arange(LANES)[:, None] * CMP_STRIDE + jnp.arange(CMP_LEN)[None, :]
    overlap = jnp.mean(((cmp_tok[:, :, None] // SLC_LEN) == jnp.arange(LANES)[None, None, :]).astype(F32), axis=1)
    overlap = jnp.where(jnp.arange(LANES)[None, :] < n_slc, overlap, 0.0).astype(BF16)
    overlap = overlap[:, :(n_slc + 7) // 8 * 8]
    e_t = ((jnp.arange(s)[:, None] // SLC_LEN) == jnp.arange(LANES)[None, :]).astype(BF16)

    tm = _tile(t, (512, 256, 128))
    tm_big = _tile(s, (1024, 512, 256, 128))
    tq = _tile(s, (256, 128))
    tk_att = _tile(s, (512, 256, 128))
    hq = MLA_NOPE_DIM + 2 * MLA_ROPE_DIM

    xf = x.reshape(t, d)
    for i in range(depth):
        wi = w_in[i]
        kr_w = wi[:, o_kr:o_kr + MLA_ROPE_DIM]
        w_in2 = jnp.concatenate(
            [wi[:, o_q:o_q + nsa_q], wi[:, o_kv:o_kv + nsa_kv], wi[:, o_cq:o_cq + q_rank],
             wi[:, o_ckv:o_ckv + kv_rank], kr_w, _rot_half_cols(kr_w), wi[:, o_gate:o_gate + n_gate],
             jnp.zeros((d, LANES - n_gate), F32)], axis=1).astype(BF16)
        proj = _mm([xf], [w_in2], gain=attn_norm[i], tm=tm_big, tn=_tile(n_in, (896, 512, 256, 128)),
                   out_dtype=F32, name="in_proj")
        proj3 = proj.reshape(b, s, n_in)

        qr, kc, vc, ks, vs, kw, vw = _nsa_prep(proj3, cos_t, sin_t, ts=_tile(s, (512, 256, 128)))
        k_cmp = _compress(kc, cmp_k_pos[i], cmp_k_w1[i], cmp_k_w2[i])
        v_cmp = _compress(vc, cmp_v_pos[i], cmp_v_w1[i], cmp_v_w2[i])
        o_cmp, sel = _cmp_sel(qr, k_cmp, v_cmp, overlap, tq=tq, n_cmp=n_cmp, n_slc=n_slc, top_n=top_n)
        o_nsa = _nsa_attn(qr, ks, vs, kw, vw, sel, e_t, o_cmp, proj3, c_gate // LANES, tq=tq, tk=tk_att)

        wq = mla_w_uq[i].reshape(q_rank, MLA_HEADS, MLA_NOPE_DIM + MLA_ROPE_DIM)
        wq_rope = wq[..., MLA_NOPE_DIM:]
        wq2 = jnp.concatenate([wq, _rot_half_cols(wq_rope)], axis=-1).reshape(q_rank, MLA_HEADS * hq).astype(BF16)
        q2 = _mm([proj], [wq2], x_cols=[c_cq // q_rank], gain=mla_q_norm[i], tab=tab_q, tm=tm_big,
                 tn=MLA_HEADS * hq, out_dtype=BF16, name="mla_q_up")
        kv_up = _mm([proj], [mla_w_ukv[i].astype(BF16)], x_cols=[c_ckv // kv_rank], gain=mla_kv_norm[i],
                    tm=tm_big, tn=mla_w_ukv.shape[2], out_dtype=BF16, name="mla_kv_up")
        o_mla = _mla_attn(q2.reshape(b, s, -1), kv_up.reshape(b, s, -1), proj3, c_kr // LANES, tab_k, tq=tk_att,
                          tk=tk_att, heads=2)

        wo = w_out[i].astype(BF16)
        xf = _mm([o_nsa.reshape(t, -1), o_mla.reshape(t, -1)], [wo[:nsa_q], wo[nsa_q:]], res=xf, tm=tm_big,
                 tn=_tile(d, (1024, 512, 256, 128)), out_dtype=F32, name="out_proj")

        j = i // 2
        if i % 2 == 0:
            xf = _ffn(xf, ffn_norm[i], dense_w1[j].astype(BF16), dense_w3[j].astype(BF16),
                      dense_w2[j].astype(BF16), tm=tm, tf=_tile(dense_w1.shape[2], (512, 256, 128)))
        else:
            route = _router(xf, ffn_norm[i], router[j], tm=tm)
            n_exp = router.shape[2]
            pair_expert = route[:, :TOP_K].astype(jnp.int32).reshape(-1)
            onehot = (pair_expert[:, None] == jnp.arange(n_exp)[None, :]).astype(jnp.int32)
            csum = jnp.cumsum(onehot, axis=0)
            rank = jnp.sum(onehot * (csum - 1), axis=1)
            group_end = jnp.cumsum((csum[-1] + tm - 1) // tm * tm)
            group_start = group_end - (csum[-1] + tm - 1) // tm * tm
            slot = (group_start[pair_expert] + rank).astype(jnp.int32)
            n_tiles = TOP_K * t // tm + n_exp
            tile_start = jnp.arange(n_tiles) * tm
            tile_expert = jnp.minimum(jnp.sum(tile_start[:, None] >= group_end[None, :], axis=1),
                                      n_exp - 1).astype(jnp.int32)
            tile_live = (tile_start < group_end[-1]).astype(jnp.int32)
            pair_token = jnp.arange(TOP_K * t, dtype=jnp.int32) // TOP_K
            slot_token = jnp.zeros((n_tiles * tm,), jnp.int32).at[slot].set(pair_token)
            x_sorted = _row_gather(xf, slot_token, chunk=tm, name="moe_gather")
            y_sorted = _moe_ffn(x_sorted, ffn_norm[i], moe_w1[j].astype(BF16), moe_w3[j].astype(BF16),
                                moe_w2[j].astype(BF16), tile_expert, tile_live, tm=tm,
                                tf=_tile(moe_w1.shape[3], (1024, 512, 256, 128)))
            slot_tiles = slot.reshape(t // tm, tm, TOP_K).transpose(0, 2, 1).reshape(-1)
            last = i == depth - 1
            xf = _moe_combine(xf, y_sorted, slot_tiles, route, final_norm if last else None, tm=tm)
            if last:
                return xf.reshape(b, s, d)
    return _final_norm(xf, final_norm, tm=tm).reshape(b, s, d)
```

```python
import functools

import jax
import jax.numpy as jnp
from jax import lax
from jax.experimental import pallas as pl
from jax.experimental.pallas import tpu as pltpu

F32 = jnp.float32
BF16 = jnp.bfloat16

NSA_HEADS = 8
NSA_KV_HEADS = 2
NSA_GROUP = NSA_HEADS // NSA_KV_HEADS
CMP_LEN = 32
CMP_STRIDE = 16
SLC_LEN = 64
SLC_SHIFT = 6
SLC_TOP = 16
N_FORCED_LOCAL = 2
FORCE_SCORE = 1.0e4
WINDOW = 512
MLA_HEADS = 8
MLA_NOPE_DIM = 128
MLA_ROPE_DIM = 64
MLA_V_DIM = 128
ROPE_THETA = 10000.0
RMS_EPS = 1e-6
TOP_K = 2

LANES = 128
NEG = -1.0e30
VMEM_LIMIT = 56 * 1024 * 1024


def _params(*sem):
    return pltpu.CompilerParams(dimension_semantics=sem, vmem_limit_bytes=VMEM_LIMIT)


def _rms(xf, g):
    return xf * lax.rsqrt(jnp.mean(xf * xf, axis=-1, keepdims=True) + RMS_EPS) * g


def _dot(a, b):
    return jnp.dot(a.astype(BF16), b.astype(BF16), preferred_element_type=F32)


def _dot_nt(a, b):
    return lax.dot_general(a.astype(BF16), b.astype(BF16), (((1,), (1,)), ((), ())),
                           preferred_element_type=F32)


def _mm_kernel(*refs, n_x, has_norm, has_tab, has_res):
    x_refs = refs[:n_x]
    w_refs = refs[n_x:2 * n_x]
    pos = 2 * n_x
    g_ref = tab_ref = res_ref = None
    if has_norm:
        g_ref = refs[pos]
        pos += 1
    if has_tab:
        tab_ref = refs[pos]
        pos += 1
    if has_res:
        res_ref = refs[pos]
        pos += 1
    o_ref = refs[pos]
    if has_norm:
        hn_sc = refs[pos + 1]

        @pl.when(pl.program_id(1) == 0)
        def _():
            hn_sc[...] = _rms(x_refs[0][...].astype(F32), g_ref[...]).astype(hn_sc.dtype)

    acc = None
    for x_ref, w_ref in zip(x_refs, w_refs):
        x = hn_sc[...] if has_norm else x_ref[...]
        d = _dot(x, w_ref[...])
        acc = d if acc is None else acc + d
    if has_tab:
        acc = acc * tab_ref[...]
    if has_res:
        acc = acc + res_ref[...]
    o_ref[...] = acc.astype(o_ref.dtype)


def _mm(xs, ws, *, x_cols=None, gain=None, tab=None, res=None, tm, tn, out_dtype, name):
    n_x = len(xs)
    assert gain is None or n_x == 1
    m = xs[0].shape[0]
    n = ws[0].shape[1]
    x_cols = x_cols or [0] * n_x
    in_specs, args = [], []
    for x, w, c in zip(xs, ws, x_cols):
        in_specs.append(pl.BlockSpec((tm, w.shape[0]), lambda i, j, c=c: (i, c)))
        args.append(x)
    for w in ws:
        in_specs.append(pl.BlockSpec((w.shape[0], tn), lambda i, j: (0, j)))
        args.append(w)
    if gain is not None:
        in_specs.append(pl.BlockSpec((1, gain.shape[-1]), lambda i, j: (0, 0)))
        args.append(gain.reshape(1, -1))
    if tab is not None:
        rb, cb = tab.shape[0] // tm, tab.shape[1] // tn
        in_specs.append(pl.BlockSpec((tm, tn), lambda i, j: (i % rb, j % cb)))
        args.append(tab)
    if res is not None:
        in_specs.append(pl.BlockSpec((tm, tn), lambda i, j: (i, j)))
        args.append(res)
    return pl.pallas_call(
        functools.partial(_mm_kernel, n_x=n_x, has_norm=gain is not None, has_tab=tab is not None,
                          has_res=res is not None),
        out_shape=jax.ShapeDtypeStruct((m, n), out_dtype),
        grid=(m // tm, n // tn),
        in_specs=in_specs,
        out_specs=pl.BlockSpec((tm, tn), lambda i, j: (i, j)),
        scratch_shapes=[pltpu.VMEM((tm, ws[0].shape[0]), BF16)] if gain is not None else [],
        compiler_params=_params("parallel", "arbitrary"),
        name=name,
    )(*args)


def _nsa_prep_kernel(p_ref, cos_ref, sin_ref, q_ref, kc_ref, vc_ref, ks_ref, vs_ref, kw_ref, vw_ref):
    cos = cos_ref[...]
    sin = sin_ref[...]
    dh = LANES

    def rope(x):
        return x * cos + pltpu.roll(x, dh // 2, 1) * sin

    scale = dh ** -0.5
    for h in range(NSA_HEADS):
        q_ref[0, :, h * dh:(h + 1) * dh] = (rope(p_ref[0, :, h * dh:(h + 1) * dh]) * scale).astype(q_ref.dtype)
    base = NSA_HEADS * dh
    outs = (kc_ref, vc_ref, ks_ref, vs_ref, kw_ref, vw_ref)
    for t, o_ref in enumerate(outs):
        for g in range(NSA_KV_HEADS):
            c0 = base + (t * NSA_KV_HEADS + g) * dh
            v = p_ref[0, :, c0:c0 + dh]
            if t % 2 == 0:
                v = rope(v)
            o_ref[0, g] = v.astype(o_ref.dtype)


def _nsa_prep(proj, cos_t, sin_t, *, ts):
    b, s, _ = proj.shape
    g, dh = NSA_KV_HEADS, LANES
    width = (NSA_HEADS + 6 * g) * dh
    kv_spec = pl.BlockSpec((1, g, ts, dh), lambda bi, si: (bi, 0, si, 0))
    kv = lambda dt: jax.ShapeDtypeStruct((b, g, s, dh), dt)
    return pl.pallas_call(
        _nsa_prep_kernel,
        out_shape=(jax.ShapeDtypeStruct((b, s, NSA_HEADS * dh), BF16),
                   kv(F32), kv(F32), kv(BF16), kv(BF16), kv(BF16), kv(BF16)),
        grid=(b, s // ts),
        in_specs=[pl.BlockSpec((1, ts, width), lambda bi, si: (bi, si, 0)),
                  pl.BlockSpec((ts, dh), lambda bi, si: (si, 0)),
                  pl.BlockSpec((ts, dh), lambda bi, si: (si, 0))],
        out_specs=(pl.BlockSpec((1, ts, NSA_HEADS * dh), lambda bi, si: (bi, si, 0)),
                   kv_spec, kv_spec, kv_spec, kv_spec, kv_spec, kv_spec),
        compiler_params=_params("parallel", "parallel"),
        name="nsa_prep",
    )(proj, cos_t, sin_t)


def _compress_kernel(t_ref, pa_ref, pb_ref, w1a_ref, w1b_ref, w2_ref, o_ref):
    t = t_ref[0, 0]
    ya = _dot(t + pa_ref[...], w1a_ref[...])
    yb = _dot(t + pb_ref[...], w1b_ref[...])
    n = yb.shape[0]
    pre = ya + pltpu.roll(yb, n - 1, 0)
    o_ref[0, 0] = _dot(jax.nn.gelu(pre), w2_ref[...])


def _compress(t, pos_emb, w1, w2):
    b, g, s, dh = t.shape
    nb = s // CMP_STRIDE
    half = CMP_STRIDE * dh
    hid = w1.shape[1]
    t16 = t.reshape(b, g, nb, half)
    pa = pos_emb[:CMP_STRIDE].reshape(1, half)
    pb = pos_emb[CMP_STRIDE:].reshape(1, half)
    const = lambda shape: pl.BlockSpec(shape, lambda bi, gi: (0, 0))
    return pl.pallas_call(
        _compress_kernel,
        out_shape=jax.ShapeDtypeStruct((b, g, nb, dh), F32),
        grid=(b, g),
        in_specs=[pl.BlockSpec((1, 1, nb, half), lambda bi, gi: (bi, gi, 0, 0)),
                  const((1, half)), const((1, half)),
                  const((half, hid)), const((half, hid)), const((hid, dh))],
        out_specs=pl.BlockSpec((1, 1, nb, dh), lambda bi, gi: (bi, gi, 0, 0)),
        compiler_params=_params("parallel", "parallel"),
        name="nsa_compress",
    )(t16, pa, pb, w1[:half].astype(BF16), w1[half:].astype(BF16), w2.astype(BF16))


def _cmp_sel_kernel(q_ref, kc_ref, vc_ref, ov_ref, ocmp_ref, sel_ref, *, tq, n_cmp, n_slc, top_n):
    qi = pl.program_id(2)
    dh = LANES
    n_c = kc_ref.shape[2]
    n_j = ov_ref.shape[1]
    pos = qi * tq + lax.broadcasted_iota(jnp.int32, (n_c, tq), 1)
    c = lax.broadcasted_iota(jnp.int32, (n_c, tq), 0)
    valid = (c * CMP_STRIDE + (CMP_LEN - 1) <= pos) & (c < n_cmp)
    kc = kc_ref[0, 0]
    vc = vc_ref[0, 0]
    psum = jnp.zeros((n_c, tq), F32)
    for r in range(NSA_GROUP):
        s = jnp.where(valid, _dot_nt(kc, q_ref[0, :, r * dh:(r + 1) * dh]), NEG)
        m = jnp.max(s, axis=0, keepdims=True)
        p = jnp.where(valid, jnp.exp(s - m), 0.0)
        p = p / jnp.maximum(jnp.sum(p, axis=0, keepdims=True), 1e-30)
        ocmp_ref[0, 0, r * dh:(r + 1) * dh, :] = _dot_tn(vc, p)
        psum = psum + p
    hi = psum.astype(BF16)
    lo = psum - hi.astype(F32)
    imp = _dot_tn(ov_ref[...], hi) + _dot_tn(ov_ref[...], lo)
    pos = qi * tq + lax.broadcasted_iota(jnp.int32, (n_j, tq), 1)
    j = lax.broadcasted_iota(jnp.int32, (n_j, tq), 0)
    causal_blk = j * SLC_LEN <= pos
    dist = lax.shift_right_logical(pos, SLC_SHIFT) - j
    forced = (j == 0) | ((dist >= 0) & (dist < N_FORCED_LOCAL))
    score = jnp.where(causal_blk, jnp.where(forced, FORCE_SCORE, imp), -jnp.inf)
    rank = jnp.zeros((n_j, tq), jnp.int32)
    for i in range(n_slc):
        row = score[i:i + 1, :]
        beats = (row > score) | ((row == score) & (j > i))
        rank = rank + jnp.where(beats, 1, 0)
    bias = jnp.where((rank < top_n) | (j >= n_slc), 0.0, NEG)
    if n_j < LANES:
        bias = jnp.concatenate([bias, jnp.zeros((LANES - n_j, tq), F32)], axis=0)
    sel_ref[0, 0] = bias.T.astype(sel_ref.dtype)


def _cmp_sel(qr, k_cmp, v_cmp, overlap, *, tq, n_cmp, n_slc, top_n):
    b, s, _ = qr.shape
    g = NSA_KV_HEADS
    gw = NSA_GROUP * LANES
    nb = k_cmp.shape[2]
    return pl.pallas_call(
        functools.partial(_cmp_sel_kernel, tq=tq, n_cmp=n_cmp, n_slc=n_slc, top_n=top_n),
        out_shape=(jax.ShapeDtypeStruct((b, g, gw, s), F32),
                   jax.ShapeDtypeStruct((b, g, s, LANES), BF16)),
        grid=(b, g, s // tq),
        in_specs=[pl.BlockSpec((1, tq, gw), lambda bi, gi, qi: (bi, qi, gi)),
                  pl.BlockSpec((1, 1, nb, LANES), lambda bi, gi, qi: (bi, gi, 0, 0)),
                  pl.BlockSpec((1, 1, nb, LANES), lambda bi, gi, qi: (bi, gi, 0, 0)),
                  pl.BlockSpec(overlap.shape, lambda bi, gi, qi: (0, 0))],
        out_specs=(pl.BlockSpec((1, 1, gw, tq), lambda bi, gi, qi: (bi, gi, 0, qi)),
                   pl.BlockSpec((1, 1, tq, LANES), lambda bi, gi, qi: (bi, gi, qi, 0))),
        compiler_params=_params("parallel", "parallel", "parallel"),
        name="nsa_cmp_select",
    )(qr, k_cmp, v_cmp, overlap)


def _flash_init(m_sc, l_sc, acc_sc):
    m_sc[...] = jnp.full(m_sc.shape, NEG, F32)
    l_sc[...] = jnp.zeros(l_sc.shape, F32)
    acc_sc[...] = jnp.zeros(acc_sc.shape, F32)


def _dot_tn(a, b):
    return lax.dot_general(a.astype(BF16), b.astype(BF16), (((0,), (0,)), ((), ())),
                           preferred_element_type=F32)


def _flash_step(qs, ks, vs, m_sc, l_sc, acc_sc, mask=None):
    chains = range(len(qs))
    scores = [_dot_nt(ks[c], qs[c]) for c in chains]
    probs = []
    for c in chains:
        s = scores[c] if mask is None else jnp.where(mask, scores[c], NEG)
        m_prev = m_sc[c]
        m_new = jnp.maximum(m_prev, jnp.max(s, axis=0, keepdims=True))
        alpha = jnp.exp(m_prev - m_new)
        p = jnp.exp(s - m_new)
        l_sc[c] = alpha * l_sc[c] + jnp.sum(p, axis=0, keepdims=True)
        m_sc[c] = m_new
        probs.append((alpha, p.astype(BF16)))
    for c in chains:
        alpha, p = probs[c]
        acc_sc[c] = alpha * acc_sc[c] + _dot_tn(vs[c], p)


def _nsa_attn_kernel(q_ref, ks_ref, vs_ref, kw_ref, vw_ref, bias_ref, et_ref, ocmp_ref, gate_ref, o_ref,
                     kaug_sc, qaug_sc, m_sc, l_sc, acc_sc, *, tq, tk):
    gi = pl.program_id(1)
    qi = pl.program_id(2)
    dh = LANES
    s_len = kaug_sc.shape[0]
    q0 = qi * tq

    @pl.when(qi == 0)
    def _():
        kaug_sc[:, :dh] = ks_ref[0, 0]
        kaug_sc[:, dh:] = et_ref[...]

    bias = bias_ref[0, 0]
    for r in range(NSA_GROUP):
        qaug_sc[r * tq:(r + 1) * tq, :dh] = q_ref[0, :, r * dh:(r + 1) * dh]
        qaug_sc[r * tq:(r + 1) * tq, dh:] = bias

    chains = m_sc.shape[0]
    n_q = NSA_GROUP * tq // chains

    def q_chain(c, width):
        return qaug_sc[c * n_q:(c + 1) * n_q, :width]

    def attend(k, v, width, mask):
        _flash_step([q_chain(c, width) for c in range(chains)], [k] * chains, [v] * chains,
                    m_sc, l_sc, acc_sc, mask=mask)

    def result():
        return [acc_sc[c] / l_sc[c] for c in range(chains)]

    def q_pos(n_keys):
        return q0 + (lax.broadcasted_iota(jnp.int32, (n_keys, n_q), 1) & (tq - 1))

    _flash_init(m_sc, l_sc, acc_sc)
    n_k = (q0 + tq + tk - 1) // tk

    def slc_full(j, carry):
        k0 = pl.multiple_of(j * tk, tk)
        attend(kaug_sc[pl.ds(k0, tk), :], vs_ref[0, 0, pl.ds(k0, tk), :], 2 * dh, None)
        return carry

    lax.fori_loop(0, n_k - 1, slc_full, 0)
    k0 = pl.multiple_of((n_k - 1) * tk, tk)
    causal = k0 + lax.broadcasted_iota(jnp.int32, (tk, n_q), 0) <= q_pos(tk)
    attend(kaug_sc[pl.ds(k0, tk), :], vs_ref[0, 0, pl.ds(k0, tk), :], 2 * dh, causal)
    o_slc = result()

    band = WINDOW + tq
    start = pl.multiple_of(jnp.clip(q0 - WINDOW, 0, s_len - band), tq)
    diff = q_pos(band) - (start + lax.broadcasted_iota(jnp.int32, (band, n_q), 0))
    _flash_init(m_sc, l_sc, acc_sc)
    attend(kw_ref[0, 0, pl.ds(start, band), :], vw_ref[0, 0, pl.ds(start, band), :], dh,
           (diff >= 0) & (diff < WINDOW))
    o_win = result()

    gates_t = jax.nn.sigmoid(gate_ref[0]).T
    n_gate = 3 * NSA_GROUP

    def gate(r, branch):
        out = None
        for g in range(NSA_KV_HEADS):
            c = g * n_gate + r * 3 + branch
            v = gates_t[c:c + 1, :]
            out = v if out is None else jnp.where(gi == g, v, out)
        return out

    per_chain = NSA_GROUP // chains
    for r in range(NSA_GROUP):
        c = r // per_chain
        cols = slice((r % per_chain) * tq, (r % per_chain + 1) * tq)
        mixed_t = (gate(r, 0) * ocmp_ref[0, 0, r * dh:(r + 1) * dh, :] + gate(r, 1) * o_slc[c][:, cols]
                   + gate(r, 2) * o_win[c][:, cols])
        o_ref[0, :, r * dh:(r + 1) * dh] = mixed_t.T.astype(o_ref.dtype)


def _nsa_attn(qr, ks, vs, kw, vw, bias, e_t, o_cmp, proj, gate_col_block, *, tq, tk, chains):
    b, s, _ = qr.shape
    g = NSA_KV_HEADS
    gw = NSA_GROUP * LANES
    assert tk % tq == 0 and s % tk == 0 and s >= WINDOW + tq and WINDOW % tq == 0 and tq & (tq - 1) == 0
    assert NSA_GROUP % chains == 0
    kv_spec = pl.BlockSpec((1, 1, s, LANES), lambda bi, gi, qi: (bi, gi, 0, 0))
    q_spec = pl.BlockSpec((1, tq, gw), lambda bi, gi, qi: (bi, qi, gi))
    rows = NSA_GROUP * tq
    return pl.pallas_call(
        functools.partial(_nsa_attn_kernel, tq=tq, tk=tk),
        out_shape=jax.ShapeDtypeStruct((b, s, NSA_HEADS * LANES), BF16),
        grid=(b, g, s // tq),
        in_specs=[q_spec, kv_spec, kv_spec, kv_spec, kv_spec,
                  pl.BlockSpec((1, 1, tq, LANES), lambda bi, gi, qi: (bi, gi, qi, 0)),
                  pl.BlockSpec((s, LANES), lambda bi, gi, qi: (0, 0)),
                  pl.BlockSpec((1, 1, gw, tq), lambda bi, gi, qi: (bi, gi, 0, qi)),
                  pl.BlockSpec((1, tq, LANES), lambda bi, gi, qi: (bi, qi, gate_col_block))],
        out_specs=q_spec,
        scratch_shapes=[pltpu.VMEM((s, 2 * LANES), BF16), pltpu.VMEM((rows, 2 * LANES), BF16),
                        pltpu.VMEM((chains, 1, rows // chains), F32), pltpu.VMEM((chains, 1, rows // chains), F32),
                        pltpu.VMEM((chains, LANES, rows // chains), F32)],
        compiler_params=_params("parallel", "parallel", "arbitrary"),
        name="nsa_slc_win",
    )(qr, ks, vs, kw, vw, bias, e_t, o_cmp, proj)


def _mla_attn_kernel(q_ref, kv_ref, kr_ref, tk_ref, o_ref, kcat_sc, m_sc, l_sc, acc_sc, *, tq, tk, heads):
    qi = pl.program_id(2)
    nope, dv = MLA_NOPE_DIM, MLA_V_DIM
    hq = kcat_sc.shape[2]

    @pl.when(qi == 0)
    def _():
        prod = kr_ref[0] * tk_ref[...]
        rope2 = prod + pltpu.roll(prod, MLA_ROPE_DIM, 1)
        for h in range(heads):
            kcat_sc[h, :, :nope] = kv_ref[0, :, h * (nope + dv):h * (nope + dv) + nope]
            kcat_sc[h, :, nope:] = rope2.astype(kcat_sc.dtype)

    q0 = qi * tq
    _flash_init(m_sc, l_sc, acc_sc)
    n_k = (q0 + tq + tk - 1) // tk

    def step(k0, mask):
        _flash_step([q_ref[0, :, h * hq:(h + 1) * hq] for h in range(heads)],
                    [kcat_sc[h, pl.ds(k0, tk), :] for h in range(heads)],
                    [kv_ref[0, pl.ds(k0, tk), h * (nope + dv) + nope:(h + 1) * (nope + dv)] for h in range(heads)],
                    m_sc, l_sc, acc_sc, mask=mask)

    def full(j, carry):
        step(pl.multiple_of(j * tk, tk), None)
        return carry

    lax.fori_loop(0, n_k - 1, full, 0)
    k0 = pl.multiple_of((n_k - 1) * tk, tk)
    causal = (k0 + lax.broadcasted_iota(jnp.int32, (tk, tq), 0)
              <= q0 + lax.broadcasted_iota(jnp.int32, (tk, tq), 1))
    step(k0, causal)
    for h in range(heads):
        o_ref[0, :, h * dv:(h + 1) * dv] = (acc_sc[h] / l_sc[h]).T.astype(o_ref.dtype)


def _mla_attn(q2, kv_up, proj, kr_col_block, tab_k, *, tq, tk, heads):
    b, s, _ = q2.shape
    hq = MLA_NOPE_DIM + 2 * MLA_ROPE_DIM
    hkv = MLA_NOPE_DIM + MLA_V_DIM
    assert tk % tq == 0 and s % tk == 0 and MLA_HEADS % heads == 0
    return pl.pallas_call(
        functools.partial(_mla_attn_kernel, tq=tq, tk=tk, heads=heads),
        out_shape=jax.ShapeDtypeStruct((b, s, MLA_HEADS * MLA_V_DIM), BF16),
        grid=(b, MLA_HEADS // heads, s // tq),
        in_specs=[pl.BlockSpec((1, tq, heads * hq), lambda bi, hi, qi: (bi, qi, hi)),
                  pl.BlockSpec((1, s, heads * hkv), lambda bi, hi, qi: (bi, 0, hi)),
                  pl.BlockSpec((1, s, LANES), lambda bi, hi, qi: (bi, 0, kr_col_block)),
                  pl.BlockSpec((s, LANES), lambda bi, hi, qi: (0, 0))],
        out_specs=pl.BlockSpec((1, tq, heads * MLA_V_DIM), lambda bi, hi, qi: (bi, qi, hi)),
        scratch_shapes=[pltpu.VMEM((heads, s, hq), BF16), pltpu.VMEM((heads, 1, tq), F32),
                        pltpu.VMEM((heads, 1, tq), F32), pltpu.VMEM((heads, MLA_V_DIM, tq), F32)],
        compiler_params=_params("parallel", "parallel", "arbitrary"),
        name="mla_attn",
    )(q2, kv_up, proj, tab_k)


def _ffn_kernel(x_ref, g_ref, w1_ref, w3_ref, w2_ref, o_ref, hn_sc, acc_sc):
    f = pl.program_id(1)

    @pl.when(f == 0)
    def _():
        hn_sc[...] = _rms(x_ref[...], g_ref[...]).astype(hn_sc.dtype)
        acc_sc[...] = jnp.zeros(acc_sc.shape, F32)

    hn = hn_sc[...]
    z = jax.nn.silu(_dot(hn, w1_ref[...])) * _dot(hn, w3_ref[...])
    acc_sc[...] += _dot(z, w2_ref[...])

    @pl.when(f == pl.num_programs(1) - 1)
    def _():
        o_ref[...] = x_ref[...] + acc_sc[...]


def _ffn(x, gain, w1, w3, w2, *, tm, tf):
    t, d = x.shape
    ff = w1.shape[1]
    return pl.pallas_call(
        _ffn_kernel,
        out_shape=jax.ShapeDtypeStruct((t, d), F32),
        grid=(t // tm, ff // tf),
        in_specs=[pl.BlockSpec((tm, d), lambda i, f: (i, 0)),
                  pl.BlockSpec((1, d), lambda i, f: (0, 0)),
                  pl.BlockSpec((d, tf), lambda i, f: (0, f)),
                  pl.BlockSpec((d, tf), lambda i, f: (0, f)),
                  pl.BlockSpec((tf, d), lambda i, f: (f, 0))],
        out_specs=pl.BlockSpec((tm, d), lambda i, f: (i, 0)),
        scratch_shapes=[pltpu.VMEM((tm, d), BF16), pltpu.VMEM((tm, d), F32)],
        compiler_params=_params("parallel", "arbitrary"),
        name="dense_ffn",
    )(x, gain.reshape(1, d), w1, w3, w2)


def _row_copy(src_ref, dst_ref, sem, src_row, dst_row):
    return pltpu.make_async_copy(src_ref.at[pl.ds(src_row, 1)], dst_ref.at[pl.ds(dst_row, 1)], sem)


def _gather_rows(idx_ref, src_ref, dst_ref, sem, n_rows):
    def issue(r, carry):
        _row_copy(src_ref, dst_ref, sem, idx_ref[r], r).start()
        return carry

    def drain(r, carry):
        _row_copy(src_ref, dst_ref, sem, idx_ref[r], r).wait()
        return carry

    lax.fori_loop(0, n_rows, issue, 0, unroll=8)
    lax.fori_loop(0, n_rows, drain, 0, unroll=8)


def _moe_ffn_kernel(te_ref, tl_ref, first_ref, next_ref, x_ref, g_ref, w1_ref, w3_ref, w2_ref, y_ref,
                    xbuf, hn_sc, sems, *, tm, share):
    i = pl.program_id(0)
    f = pl.program_id(1)
    last_step = (i == pl.num_programs(0) - 1) & (f == pl.num_programs(1) - 1)
    live = tl_ref[i] > 0
    slot = lax.rem(i, 2)
    rows_started = (i == 0) | (tl_ref[jnp.maximum(i - 1, 0)] > 0)

    def wait_rows(s):
        def body(r, carry):
            _row_copy(x_ref, xbuf.at[s], sems.at[s], 0, r).wait()
            return carry

        lax.fori_loop(0, tm, body, 0, unroll=8)

    @pl.when(f == 0)
    def _():
        y_ref[...] = jnp.zeros(y_ref.shape, F32)

    @pl.when((i == 0) & (f == 0))
    def _():
        def body(r, carry):
            _row_copy(x_ref, xbuf.at[0], sems.at[0], first_ref[r], r).start()
            return carry

        lax.fori_loop(0, tm, body, 0, unroll=8)

    @pl.when(rows_started & (f == 0))
    def _():
        wait_rows(slot)

    @pl.when(live)
    def _():
        @pl.when(f == 0)
        def _():
            hn_sc[...] = _rms(xbuf[slot], g_ref[...]).astype(hn_sc.dtype)

        for r in range(share):
            row = f * share + r
            _row_copy(x_ref, xbuf.at[1 - slot], sems.at[1 - slot], next_ref[row], row).start()
        hn = hn_sc[...]
        z = jax.nn.silu(_dot(hn, w1_ref[0])) * _dot(hn, w3_ref[0])
        y_ref[...] += _dot(z, w2_ref[0])

    @pl.when(live & last_step)
    def _():
        wait_rows(1 - slot)


def _moe_ffn(x, slot_token, gain, w1, w3, w2, tile_expert, tile_live, *, tm, tf):
    d = x.shape[1]
    ff = w1.shape[2]
    n_tiles = slot_token.shape[0] // tm
    n_f = ff // tf
    assert tm % n_f == 0

    def w_col(i, f, te, tl):
        return (te[i], 0, jnp.where(tl[i] > 0, f, 0))

    def w_row(i, f, te, tl):
        return (te[i], jnp.where(tl[i] > 0, f, 0), 0)

    return pl.pallas_call(
        functools.partial(_moe_ffn_kernel, tm=tm, share=tm // n_f),
        out_shape=jax.ShapeDtypeStruct((n_tiles * tm, d), F32),
        grid_spec=pltpu.PrefetchScalarGridSpec(
            num_scalar_prefetch=2,
            grid=(n_tiles, n_f),
            in_specs=[pl.BlockSpec((tm,), lambda i, f, te, tl: (0,), memory_space=pltpu.SMEM),
                      pl.BlockSpec((tm,), lambda i, f, te, tl: (jnp.minimum(i + 1, n_tiles - 1),),
                                   memory_space=pltpu.SMEM),
                      pl.BlockSpec(memory_space=pl.ANY),
                      pl.BlockSpec((1, d), lambda i, f, te, tl: (0, 0)),
                      pl.BlockSpec((1, d, tf), w_col),
                      pl.BlockSpec((1, d, tf), w_col),
                      pl.BlockSpec((1, tf, d), w_row)],
            out_specs=pl.BlockSpec((tm, d), lambda i, f, te, tl: (i, 0)),
            scratch_shapes=[pltpu.VMEM((2, tm, d), F32), pltpu.VMEM((tm, d), BF16),
                            pltpu.SemaphoreType.DMA((2,))]),
        compiler_params=_params("arbitrary", "arbitrary"),
        name="moe_ffn",
    )(tile_expert, tile_live, slot_token, slot_token, x, gain.reshape(1, d), w1, w3, w2)


def _moe_combine_kernel(*refs, final, tm):
    if final:
        slot_ref, x_ref, y_ref, r_ref, g_ref, o_ref, ybuf, sem = refs
    else:
        slot_ref, x_ref, y_ref, r_ref, o_ref, ybuf, sem = refs
    _gather_rows(slot_ref, y_ref, ybuf, sem, TOP_K * tm)
    route = r_ref[...]
    mix = None
    for k in range(TOP_K):
        term = route[:, TOP_K + k:TOP_K + k + 1] * ybuf[k * tm:(k + 1) * tm, :]
        mix = term if mix is None else mix + term
    out = x_ref[...] + mix
    if final:
        out = _rms(out, g_ref[...])
    o_ref[...] = out


def _moe_combine(x, y_sorted, slot_tiles, route, final_gain, *, tm):
    t, d = x.shape
    in_specs = [pl.BlockSpec((TOP_K * tm,), lambda i: (i,), memory_space=pltpu.SMEM),
                pl.BlockSpec((tm, d), lambda i: (i, 0)),
                pl.BlockSpec(memory_space=pl.ANY),
                pl.BlockSpec((tm, LANES), lambda i: (i, 0))]
    args = [slot_tiles, x, y_sorted, route]
    if final_gain is not None:
        in_specs.append(pl.BlockSpec((1, d), lambda i: (0, 0)))
        args.append(final_gain.reshape(1, d))
    return pl.pallas_call(
        functools.partial(_moe_combine_kernel, final=final_gain is not None, tm=tm),
        out_shape=jax.ShapeDtypeStruct((t, d), F32),
        grid=(t // tm,),
        in_specs=in_specs,
        out_specs=pl.BlockSpec((tm, d), lambda i: (i, 0)),
        scratch_shapes=[pltpu.VMEM((TOP_K * tm, d), F32), pltpu.SemaphoreType.DMA(())],
        compiler_params=_params("arbitrary"),
        name="moe_combine",
    )(*args)


def _router_kernel(x_ref, g_ref, w_ref, o_ref, *, n_exp):
    hn = _rms(x_ref[...], g_ref[...])
    w = w_ref[...]
    h_hi = hn.astype(BF16)
    h_lo = hn - h_hi.astype(F32)
    w_hi = w.astype(BF16)
    w_lo = w - w_hi.astype(F32)
    logits = _dot(h_hi, w_hi) + _dot(h_hi, w_lo) + _dot(h_lo, w_hi)
    lane = lax.broadcasted_iota(jnp.int32, logits.shape, 1)
    valid = lane < n_exp
    lg = jnp.where(valid, logits, NEG)
    ex = jnp.where(valid, jnp.exp(lg - jnp.max(lg, axis=-1, keepdims=True)), 0.0)
    probs = ex / jnp.sum(ex, axis=-1, keepdims=True)
    rest = jnp.where(valid, probs, -1.0)
    tops = []
    for _ in range(TOP_K):
        v = jnp.max(rest, axis=-1, keepdims=True)
        idx = jnp.min(jnp.where(rest == v, lane, LANES), axis=-1, keepdims=True)
        tops.append((v, idx))
        rest = jnp.where(lane == idx, -1.0, rest)
    den = tops[0][0]
    for v, _ in tops[1:]:
        den = den + v
    route = jnp.zeros(logits.shape, F32)
    for k, (v, idx) in enumerate(tops):
        route = jnp.where(lane == k, idx.astype(F32), route)
        route = jnp.where(lane == TOP_K + k, v / den, route)
    o_ref[...] = route


def _router(x, gain, router, *, tm):
    t, d = x.shape
    n_exp = router.shape[1]
    w = jnp.zeros((d, LANES), F32).at[:, :n_exp].set(router)
    return pl.pallas_call(
        functools.partial(_router_kernel, n_exp=n_exp),
        out_shape=jax.ShapeDtypeStruct((t, LANES), F32),
        grid=(t // tm,),
        in_specs=[pl.BlockSpec((tm, d), lambda i: (i, 0)),
                  pl.BlockSpec((1, d), lambda i: (0, 0)),
                  pl.BlockSpec((d, LANES), lambda i: (0, 0))],
        out_specs=pl.BlockSpec((tm, LANES), lambda i: (i, 0)),
        compiler_params=_params("parallel"),
        name="moe_router",
    )(x, gain.reshape(1, d), w)


def _final_norm_kernel(x_ref, g_ref, o_ref):
    o_ref[...] = _rms(x_ref[...], g_ref[...])


def _final_norm(x, gain, *, tm):
    t, d = x.shape
    return pl.pallas_call(
        _final_norm_kernel,
        out_shape=jax.ShapeDtypeStruct((t, d), F32),
        grid=(t // tm,),
        in_specs=[pl.BlockSpec((tm, d), lambda i: (i, 0)), pl.BlockSpec((1, d), lambda i: (0, 0))],
        out_specs=pl.BlockSpec((tm, d), lambda i: (i, 0)),
        compiler_params=_params("parallel"),
        name="final_norm",
    )(x, gain.reshape(1, d))


def _rope_tables(seq, dim):
    inv = 1.0 / (ROPE_THETA ** (jnp.arange(0, dim, 2, dtype=F32) / dim))
    ang = jnp.arange(seq, dtype=F32)[:, None] * inv[None, :]
    return jnp.cos(ang), jnp.sin(ang)


def _rot_half_cols(w):
    half = w.shape[-1] // 2
    return jnp.concatenate([w[..., half:], w[..., :half]], axis=-1)


def _tile(n, pref):
    for t in pref:
        if n % t == 0:
            return t
    return n


def kernel(x, attn_norm, w_in, w_out, cmp_k_pos, cmp_k_w1, cmp_k_w2, cmp_v_pos, cmp_v_w1, cmp_v_w2, mla_q_norm, mla_w_uq, mla_kv_norm, mla_w_ukv, ffn_norm, dense_w1, dense_w3, dense_w2, router, moe_w1, moe_w3, moe_w2, final_norm):
    b, s, d = x.shape
    depth = w_in.shape[0]
    t = b * s
    dh = LANES
    q_rank = mla_w_uq.shape[1]
    kv_rank = mla_w_ukv.shape[1]
    nsa_q = NSA_HEADS * dh
    nsa_kv = 6 * NSA_KV_HEADS * dh
    n_gate = 3 * NSA_HEADS
    o_q, o_kv, o_gate = 0, nsa_q, nsa_q + nsa_kv
    o_cq = o_gate + n_gate
    o_ckv = o_cq + q_rank
    o_kr = o_ckv + kv_rank
    assert o_kr + MLA_ROPE_DIM == w_in.shape[2]
    assert s % SLC_LEN == 0 and s % CMP_STRIDE == 0 and s // SLC_LEN <= LANES and s // CMP_STRIDE <= LANES
    c_cq = nsa_q + nsa_kv
    c_ckv = c_cq + q_rank
    c_kr = c_ckv + kv_rank
    c_gate = c_kr + 2 * MLA_ROPE_DIM
    n_in = c_gate + LANES
    assert q_rank % LANES == 0 and kv_rank % LANES == 0 and c_cq % q_rank == 0 and c_ckv % kv_rank == 0

    n_cmp = (s - CMP_LEN) // CMP_STRIDE + 1
    n_slc = s // SLC_LEN
    top_n = min(SLC_TOP, n_slc)

    cos_n, sin_n = _rope_tables(s, dh)
    cos_t = jnp.concatenate([cos_n, cos_n], axis=-1)
    sin_t = jnp.concatenate([-sin_n, sin_n], axis=-1)
    cos_m, sin_m = _rope_tables(s, MLA_ROPE_DIM)
    rot_cos = jnp.concatenate([cos_m, cos_m], axis=-1)
    rot_sin = jnp.concatenate([-sin_m, sin_m], axis=-1)
    mla_scale = (MLA_NOPE_DIM + MLA_ROPE_DIM) ** -0.5
    tab_q = jnp.concatenate([jnp.ones((s, MLA_NOPE_DIM), F32), rot_cos, rot_sin], axis=-1) * mla_scale
    tab_q = jnp.tile(tab_q, (1, MLA_HEADS))
    tab_k = jnp.concatenate([rot_cos, rot_sin], axis=-1)

    cmp_tok = jnp.arange(LANES)[:, None] * CMP_STRIDE + jnp.arange(CMP_LEN)[None, :]
    overlap = jnp.mean(((cmp_tok[:, :, None] // SLC_LEN) == jnp.arange(LANES)[None, None, :]).astype(F32), axis=1)
    overlap = jnp.where(jnp.arange(LANES)[None, :] < n_slc, overlap, 0.0).astype(BF16)
    overlap = overlap[:, :(n_slc + 7) // 8 * 8]
    e_t = ((jnp.arange(s)[:, None] // SLC_LEN) == jnp.arange(LANES)[None, :]).astype(BF16)

    tm = _tile(t, (512, 256, 128))
    tm_big = _tile(s, (1024, 512, 256, 128))
    tq = _tile(s, (256, 128))
    tk_att = _tile(s, (512, 256, 128))
    hq = MLA_NOPE_DIM + 2 * MLA_ROPE_DIM

    xf = x.reshape(t, d)
    for i in range(depth):
        wi = w_in[i]
        kr_w = wi[:, o_kr:o_kr + MLA_ROPE_DIM]
        w_in2 = jnp.concatenate(
            [wi[:, o_q:o_q + nsa_q], wi[:, o_kv:o_kv + nsa_kv], wi[:, o_cq:o_cq + q_rank],
             wi[:, o_ckv:o_ckv + kv_rank], kr_w, _rot_half_cols(kr_w), wi[:, o_gate:o_gate + n_gate],
             jnp.zeros((d, LANES - n_gate), F32)], axis=1).astype(BF16)
        proj = _mm([xf], [w_in2], gain=attn_norm[i], tm=tm_big, tn=_tile(n_in, (896, 512, 256, 128)),
                   out_dtype=F32, name="in_proj")
        proj3 = proj.reshape(b, s, n_in)

        qr, kc, vc, ks, vs, kw, vw = _nsa_prep(proj3, cos_t, sin_t, ts=_tile(s, (512, 256, 128)))
        k_cmp = _compress(kc, cmp_k_pos[i], cmp_k_w1[i], cmp_k_w2[i])
        v_cmp = _compress(vc, cmp_v_pos[i], cmp_v_w1[i], cmp_v_w2[i])
        o_cmp, sel = _cmp_sel(qr, k_cmp, v_cmp, overlap, tq=tq, n_cmp=n_cmp, n_slc=n_slc, top_n=top_n)
        o_nsa = _nsa_attn(qr, ks, vs, kw, vw, sel, e_t, o_cmp, proj3, c_gate // LANES, tq=tq, tk=tk_att, chains=4)

        wq = mla_w_uq[i].reshape(q_rank, MLA_HEADS, MLA_NOPE_DIM + MLA_ROPE_DIM)
        wq_rope = wq[..., MLA_NOPE_DIM:]
        wq2 = jnp.concatenate([wq, _rot_half_cols(wq_rope)], axis=-1).reshape(q_rank, MLA_HEADS * hq).astype(BF16)
        q2 = _mm([proj], [wq2], x_cols=[c_cq // q_rank], gain=mla_q_norm[i], tab=tab_q, tm=tm_big,
                 tn=MLA_HEADS * hq, out_dtype=BF16, name="mla_q_up")
        kv_up = _mm([proj], [mla_w_ukv[i].astype(BF16)], x_cols=[c_ckv // kv_rank], gain=mla_kv_norm[i],
                    tm=tm_big, tn=mla_w_ukv.shape[2], out_dtype=BF16, name="mla_kv_up")
        o_mla = _mla_attn(q2.reshape(b, s, -1), kv_up.reshape(b, s, -1), proj3, c_kr // LANES, tab_k, tq=tk_att,
                          tk=tk_att, heads=4)

        wo = w_out[i].astype(BF16)
        xf = _mm([o_nsa.reshape(t, -1), o_mla.reshape(t, -1)], [wo[:nsa_q], wo[nsa_q:]], res=xf, tm=tm_big,
                 tn=_tile(d, (1024, 512, 256, 128)), out_dtype=F32, name="out_proj")

        j = i // 2
        if i % 2 == 0:
            xf = _ffn(xf, ffn_norm[i], dense_w1[j].astype(BF16), dense_w3[j].astype(BF16),
                      dense_w2[j].astype(BF16), tm=tm, tf=_tile(dense_w1.shape[2], (512, 256, 128)))
        else:
            route = _router(xf, ffn_norm[i], router[j], tm=tm)
            n_exp = router.shape[2]
            pair_expert = route[:, :TOP_K].astype(jnp.int32).reshape(-1)
            onehot = (pair_expert[:, None] == jnp.arange(n_exp)[None, :]).astype(jnp.int32)
            csum = jnp.cumsum(onehot, axis=0)
            rank = jnp.sum(onehot * (csum - 1), axis=1)
            group_end = jnp.cumsum((csum[-1] + tm - 1) // tm * tm)
            group_start = group_end - (csum[-1] + tm - 1) // tm * tm
            slot = (group_start[pair_expert] + rank).astype(jnp.int32)
            n_tiles = TOP_K * t // tm + n_exp
            tile_start = jnp.arange(n_tiles) * tm
            tile_expert = jnp.minimum(jnp.sum(tile_start[:, None] >= group_end[None, :], axis=1),
                                      n_exp - 1).astype(jnp.int32)
            tile_live = (tile_start < group_end[-1]).astype(jnp.int32)
            pair_token = jnp.arange(TOP_K * t, dtype=jnp.int32) // TOP_K
            slot_token = jnp.zeros((n_tiles * tm,), jnp.int32).at[slot].set(pair_token)
            ff_e = moe_w1.shape[3]
            tf_e = next(c for c in (896, 1024, 512, 256, 128, ff_e) if ff_e % c == 0 and tm % (ff_e // c) == 0)
            y_sorted = _moe_ffn(xf, slot_token, ffn_norm[i], moe_w1[j].astype(BF16), moe_w3[j].astype(BF16),
                                moe_w2[j].astype(BF16), tile_expert, tile_live, tm=tm, tf=tf_e)
            slot_tiles = slot.reshape(t // tm, tm, TOP_K).transpose(0, 2, 1).reshape(-1)
            last = i == depth - 1
            xf = _moe_combine(xf, y_sorted, slot_tiles, route, final_norm if last else None, tm=tm)
            if last:
                return xf.reshape(b, s, d)
    return _final_norm(xf, final_norm, tm=tm).reshape(b, s, d)
```

```python
import functools

import jax
import jax.numpy as jnp
from jax import lax
from jax.experimental import pallas as pl
from jax.experimental.pallas import tpu as pltpu

F32 = jnp.float32
BF16 = jnp.bfloat16

NSA_HEADS = 8
NSA_KV_HEADS = 2
NSA_GROUP = NSA_HEADS // NSA_KV_HEADS
CMP_LEN = 32
CMP_STRIDE = 16
SLC_LEN = 64
SLC_SHIFT = 6
SLC_TOP = 16
N_FORCED_LOCAL = 2
FORCE_SCORE = 1.0e4
WINDOW = 512
MLA_HEADS = 8
MLA_NOPE_DIM = 128
MLA_ROPE_DIM = 64
MLA_V_DIM = 128
ROPE_THETA = 10000.0
RMS_EPS = 1e-6
TOP_K = 2

LANES = 128
NEG = -1.0e30
VMEM_LIMIT = 56 * 1024 * 1024


def _params(*sem):
    return pltpu.CompilerParams(dimension_semantics=sem, vmem_limit_bytes=VMEM_LIMIT)


def _rms(xf, g):
    return xf * lax.rsqrt(jnp.mean(xf * xf, axis=-1, keepdims=True) + RMS_EPS) * g


def _dot(a, b):
    return jnp.dot(a.astype(BF16), b.astype(BF16), preferred_element_type=F32)


def _dot_nt(a, b):
    return lax.dot_general(a.astype(BF16), b.astype(BF16), (((1,), (1,)), ((), ())),
                           preferred_element_type=F32)


def _mm_kernel(*refs, n_x, has_norm, has_tab, has_res):
    x_refs = refs[:n_x]
    w_refs = refs[n_x:2 * n_x]
    pos = 2 * n_x
    g_ref = tab_ref = res_ref = None
    if has_norm:
        g_ref = refs[pos]
        pos += 1
    if has_tab:
        tab_ref = refs[pos]
        pos += 1
    if has_res:
        res_ref = refs[pos]
        pos += 1
    o_ref = refs[pos]
    if has_norm:
        hn_sc = refs[pos + 1]

        @pl.when(pl.program_id(1) == 0)
        def _():
            hn_sc[...] = _rms(x_refs[0][...].astype(F32), g_ref[...]).astype(hn_sc.dtype)

    acc = None
    for x_ref, w_ref in zip(x_refs, w_refs):
        x = hn_sc[...] if has_norm else x_ref[...]
        d = _dot(x, w_ref[...])
        acc = d if acc is None else acc + d
    if has_tab:
        acc = acc * tab_ref[...]
    if has_res:
        acc = acc + res_ref[...]
    o_ref[...] = acc.astype(o_ref.dtype)


def _mm(xs, ws, *, x_cols=None, gain=None, tab=None, res=None, tm, tn, out_dtype, name):
    n_x = len(xs)
    assert gain is None or n_x == 1
    m = xs[0].shape[0]
    n = ws[0].shape[1]
    x_cols = x_cols or [0] * n_x
    in_specs, args = [], []
    for x, w, c in zip(xs, ws, x_cols):
        in_specs.append(pl.BlockSpec((tm, w.shape[0]), lambda i, j, c=c: (i, c)))
        args.append(x)
    for w in ws:
        in_specs.append(pl.BlockSpec((w.shape[0], tn), lambda i, j: (0, j)))
        args.append(w)
    if gain is not None:
        in_specs.append(pl.BlockSpec((1, gain.shape[-1]), lambda i, j: (0, 0)))
        args.append(gain.reshape(1, -1))
    if tab is not None:
        rb, cb = tab.shape[0] // tm, tab.shape[1] // tn
        in_specs.append(pl.BlockSpec((tm, tn), lambda i, j: (i % rb, j % cb)))
        args.append(tab)
    if res is not None:
        in_specs.append(pl.BlockSpec((tm, tn), lambda i, j: (i, j)))
        args.append(res)
    return pl.pallas_call(
        functools.partial(_mm_kernel, n_x=n_x, has_norm=gain is not None, has_tab=tab is not None,
                          has_res=res is not None),
        out_shape=jax.ShapeDtypeStruct((m, n), out_dtype),
        grid=(m // tm, n // tn),
        in_specs=in_specs,
        out_specs=pl.BlockSpec((tm, tn), lambda i, j: (i, j)),
        scratch_shapes=[pltpu.VMEM((tm, ws[0].shape[0]), BF16)] if gain is not None else [],
        compiler_params=_params("parallel", "arbitrary"),
        name=name,
    )(*args)


def _nsa_prep_kernel(p_ref, cos_ref, sin_ref, q_ref, kc_ref, vc_ref, ks_ref, vs_ref, kw_ref, vw_ref):
    cos = cos_ref[...]
    sin = sin_ref[...]
    dh = LANES

    def rope(x):
        return x * cos + pltpu.roll(x, dh // 2, 1) * sin

    scale = dh ** -0.5
    for h in range(NSA_HEADS):
        q_ref[0, :, h * dh:(h + 1) * dh] = (rope(p_ref[0, :, h * dh:(h + 1) * dh]) * scale).astype(q_ref.dtype)
    base = NSA_HEADS * dh
    outs = (kc_ref, vc_ref, ks_ref, vs_ref, kw_ref, vw_ref)
    for t, o_ref in enumerate(outs):
        for g in range(NSA_KV_HEADS):
            c0 = base + (t * NSA_KV_HEADS + g) * dh
            v = p_ref[0, :, c0:c0 + dh]
            if t % 2 == 0:
                v = rope(v)
            o_ref[0, g] = v.astype(o_ref.dtype)


def _nsa_prep(proj, cos_t, sin_t, *, ts):
    b, s, _ = proj.shape
    g, dh = NSA_KV_HEADS, LANES
    width = (NSA_HEADS + 6 * g) * dh
    kv_spec = pl.BlockSpec((1, g, ts, dh), lambda bi, si: (bi, 0, si, 0))
    kv = lambda dt: jax.ShapeDtypeStruct((b, g, s, dh), dt)
    return pl.pallas_call(
        _nsa_prep_kernel,
        out_shape=(jax.ShapeDtypeStruct((b, s, NSA_HEADS * dh), BF16),
                   kv(F32), kv(F32), kv(BF16), kv(BF16), kv(BF16), kv(BF16)),
        grid=(b, s // ts),
        in_specs=[pl.BlockSpec((1, ts, width), lambda bi, si: (bi, si, 0)),
                  pl.BlockSpec((ts, dh), lambda bi, si: (si, 0)),
                  pl.BlockSpec((ts, dh), lambda bi, si: (si, 0))],
        out_specs=(pl.BlockSpec((1, ts, NSA_HEADS * dh), lambda bi, si: (bi, si, 0)),
                   kv_spec, kv_spec, kv_spec, kv_spec, kv_spec, kv_spec),
        compiler_params=_params("parallel", "parallel"),
        name="nsa_prep",
    )(proj, cos_t, sin_t)


def _compress_kernel(t_ref, pa_ref, pb_ref, w1a_ref, w1b_ref, w2_ref, o_ref):
    t = t_ref[0, 0]
    ya = _dot(t + pa_ref[...], w1a_ref[...])
    yb = _dot(t + pb_ref[...], w1b_ref[...])
    n = yb.shape[0]
    pre = ya + pltpu.roll(yb, n - 1, 0)
    o_ref[0, 0] = _dot(jax.nn.gelu(pre), w2_ref[...])


def _compress(t, pos_emb, w1, w2):
    b, g, s, dh = t.shape
    nb = s // CMP_STRIDE
    half = CMP_STRIDE * dh
    hid = w1.shape[1]
    t16 = t.reshape(b, g, nb, half)
    pa = pos_emb[:CMP_STRIDE].reshape(1, half)
    pb = pos_emb[CMP_STRIDE:].reshape(1, half)
    const = lambda shape: pl.BlockSpec(shape, lambda bi, gi: (0, 0))
    return pl.pallas_call(
        _compress_kernel,
        out_shape=jax.ShapeDtypeStruct((b, g, nb, dh), F32),
        grid=(b, g),
        in_specs=[pl.BlockSpec((1, 1, nb, half), lambda bi, gi: (bi, gi, 0, 0)),
                  const((1, half)), const((1, half)),
                  const((half, hid)), const((half, hid)), const((hid, dh))],
        out_specs=pl.BlockSpec((1, 1, nb, dh), lambda bi, gi: (bi, gi, 0, 0)),
        compiler_params=_params("parallel", "parallel"),
        name="nsa_compress",
    )(t16, pa, pb, w1[:half].astype(BF16), w1[half:].astype(BF16), w2.astype(BF16))


def _cmp_sel_kernel(q_ref, kc_ref, vc_ref, ov_ref, ocmp_ref, sel_ref, *, tq, n_cmp, n_slc, top_n):
    qi = pl.program_id(2)
    dh = LANES
    n_c = kc_ref.shape[2]
    n_j = ov_ref.shape[1]
    pos = qi * tq + lax.broadcasted_iota(jnp.int32, (n_c, tq), 1)
    c = lax.broadcasted_iota(jnp.int32, (n_c, tq), 0)
    valid = (c * CMP_STRIDE + (CMP_LEN - 1) <= pos) & (c < n_cmp)
    kc = kc_ref[0, 0]
    vc = vc_ref[0, 0]
    psum = jnp.zeros((n_c, tq), F32)
    for r in range(NSA_GROUP):
        s = jnp.where(valid, _dot_nt(kc, q_ref[0, :, r * dh:(r + 1) * dh]), NEG)
        m = jnp.max(s, axis=0, keepdims=True)
        p = jnp.where(valid, jnp.exp(s - m), 0.0)
        p = p / jnp.maximum(jnp.sum(p, axis=0, keepdims=True), 1e-30)
        ocmp_ref[0, 0, r * dh:(r + 1) * dh, :] = _dot_tn(vc, p)
        psum = psum + p
    hi = psum.astype(BF16)
    lo = psum - hi.astype(F32)
    imp = _dot_tn(ov_ref[...], hi) + _dot_tn(ov_ref[...], lo)
    pos = qi * tq + lax.broadcasted_iota(jnp.int32, (n_j, tq), 1)
    j = lax.broadcasted_iota(jnp.int32, (n_j, tq), 0)
    causal_blk = j * SLC_LEN <= pos
    dist = lax.shift_right_logical(pos, SLC_SHIFT) - j
    forced = (j == 0) | ((dist >= 0) & (dist < N_FORCED_LOCAL))
    score = jnp.where(causal_blk, jnp.where(forced, FORCE_SCORE, imp), -jnp.inf)
    rank = jnp.zeros((n_j, tq), jnp.int32)
    for i in range(n_slc):
        row = score[i:i + 1, :]
        beats = (row > score) | ((row == score) & (j > i))
        rank = rank + jnp.where(beats, 1, 0)
    bias = jnp.where((rank < top_n) | (j >= n_slc), 0.0, NEG)
    if n_j < LANES:
        bias = jnp.concatenate([bias, jnp.zeros((LANES - n_j, tq), F32)], axis=0)
    sel_ref[0, 0] = bias.T.astype(sel_ref.dtype)


def _cmp_sel(qr, k_cmp, v_cmp, overlap, *, tq, n_cmp, n_slc, top_n):
    b, s, _ = qr.shape
    g = NSA_KV_HEADS
    gw = NSA_GROUP * LANES
    nb = k_cmp.shape[2]
    return pl.pallas_call(
        functools.partial(_cmp_sel_kernel, tq=tq, n_cmp=n_cmp, n_slc=n_slc, top_n=top_n),
        out_shape=(jax.ShapeDtypeStruct((b, g, gw, s), F32),
                   jax.ShapeDtypeStruct((b, g, s, LANES), BF16)),
        grid=(b, g, s // tq),
        in_specs=[pl.BlockSpec((1, tq, gw), lambda bi, gi, qi: (bi, qi, gi)),
                  pl.BlockSpec((1, 1, nb, LANES), lambda bi, gi, qi: (bi, gi, 0, 0)),
                  pl.BlockSpec((1, 1, nb, LANES), lambda bi, gi, qi: (bi, gi, 0, 0)),
                  pl.BlockSpec(overlap.shape, lambda bi, gi, qi: (0, 0))],
        out_specs=(pl.BlockSpec((1, 1, gw, tq), lambda bi, gi, qi: (bi, gi, 0, qi)),
                   pl.BlockSpec((1, 1, tq, LANES), lambda bi, gi, qi: (bi, gi, qi, 0))),
        compiler_params=_params("parallel", "parallel", "parallel"),
        name="nsa_cmp_select",
    )(qr, k_cmp, v_cmp, overlap)


def _flash_init(m_sc, l_sc, acc_sc):
    m_sc[...] = jnp.full(m_sc.shape, NEG, F32)
    l_sc[...] = jnp.zeros(l_sc.shape, F32)
    acc_sc[...] = jnp.zeros(acc_sc.shape, F32)


def _dot_tn(a, b):
    return lax.dot_general(a.astype(BF16), b.astype(BF16), (((0,), (0,)), ((), ())),
                           preferred_element_type=F32)


def _flash_step(qs, ks, vs, m_sc, l_sc, acc_sc, mask=None):
    chains = range(len(qs))
    scores = [_dot_nt(ks[c], qs[c]) for c in chains]
    probs = []
    for c in chains:
        s = scores[c] if mask is None else jnp.where(mask, scores[c], NEG)
        m_prev = m_sc[c]
        m_new = jnp.maximum(m_prev, jnp.max(s, axis=0, keepdims=True))
        alpha = jnp.exp(m_prev - m_new)
        p = jnp.exp(s - m_new)
        l_sc[c] = alpha * l_sc[c] + jnp.sum(p, axis=0, keepdims=True)
        m_sc[c] = m_new
        probs.append((alpha, p.astype(BF16)))
    for c in chains:
        alpha, p = probs[c]
        acc_sc[c] = alpha * acc_sc[c] + _dot_tn(vs[c], p)


def _nsa_attn_kernel(q_ref, ks_ref, vs_ref, kw_ref, vw_ref, bias_ref, et_ref, ocmp_ref, gate_ref, o_ref,
                     kaug_sc, qaug_sc, m_sc, l_sc, acc_sc, *, tq, tk):
    gi = pl.program_id(1)
    qi = pl.program_id(2)
    dh = LANES
    s_len = kaug_sc.shape[0]
    q0 = qi * tq

    @pl.when(qi == 0)
    def _():
        kaug_sc[:, :dh] = ks_ref[0, 0]
        kaug_sc[:, dh:] = et_ref[...]

    bias = bias_ref[0, 0]
    for r in range(NSA_GROUP):
        qaug_sc[r * tq:(r + 1) * tq, :dh] = q_ref[0, :, r * dh:(r + 1) * dh]
        qaug_sc[r * tq:(r + 1) * tq, dh:] = bias

    chains = m_sc.shape[0]
    n_q = NSA_GROUP * tq // chains

    def q_chain(c, width):
        return qaug_sc[c * n_q:(c + 1) * n_q, :width]

    def attend(k, v, width, mask):
        _flash_step([q_chain(c, width) for c in range(chains)], [k] * chains, [v] * chains,
                    m_sc, l_sc, acc_sc, mask=mask)

    def result():
        return [acc_sc[c] / l_sc[c] for c in range(chains)]

    def q_pos(n_keys):
        return q0 + (lax.broadcasted_iota(jnp.int32, (n_keys, n_q), 1) & (tq - 1))

    _flash_init(m_sc, l_sc, acc_sc)
    n_k = (q0 + tq + tk - 1) // tk

    def slc_full(j, carry):
        k0 = pl.multiple_of(j * tk, tk)
        attend(kaug_sc[pl.ds(k0, tk), :], vs_ref[0, 0, pl.ds(k0, tk), :], 2 * dh, None)
        return carry

    lax.fori_loop(0, n_k - 1, slc_full, 0)
    k0 = pl.multiple_of((n_k - 1) * tk, tk)
    causal = k0 + lax.broadcasted_iota(jnp.int32, (tk, n_q), 0) <= q_pos(tk)
    attend(kaug_sc[pl.ds(k0, tk), :], vs_ref[0, 0, pl.ds(k0, tk), :], 2 * dh, causal)
    o_slc = result()

    band = WINDOW + tq
    start = pl.multiple_of(jnp.clip(q0 - WINDOW, 0, s_len - band), tq)
    diff = q_pos(band) - (start + lax.broadcasted_iota(jnp.int32, (band, n_q), 0))
    _flash_init(m_sc, l_sc, acc_sc)
    attend(kw_ref[0, 0, pl.ds(start, band), :], vw_ref[0, 0, pl.ds(start, band), :], dh,
           (diff >= 0) & (diff < WINDOW))
    o_win = result()

    gates_t = jax.nn.sigmoid(gate_ref[0]).T
    n_gate = 3 * NSA_GROUP

    def gate(r, branch):
        out = None
        for g in range(NSA_KV_HEADS):
            c = g * n_gate + r * 3 + branch
            v = gates_t[c:c + 1, :]
            out = v if out is None else jnp.where(gi == g, v, out)
        return out

    per_chain = NSA_GROUP // chains
    for r in range(NSA_GROUP):
        c = r // per_chain
        cols = slice((r % per_chain) * tq, (r % per_chain + 1) * tq)
        mixed_t = (gate(r, 0) * ocmp_ref[0, 0, r * dh:(r + 1) * dh, :] + gate(r, 1) * o_slc[c][:, cols]
                   + gate(r, 2) * o_win[c][:, cols])
        o_ref[0, :, r * dh:(r + 1) * dh] = mixed_t.T.astype(o_ref.dtype)


def _nsa_attn(qr, ks, vs, kw, vw, bias, e_t, o_cmp, proj, gate_col_block, *, tq, tk, chains):
    b, s, _ = qr.shape
    g = NSA_KV_HEADS
    gw = NSA_GROUP * LANES
    assert tk % tq == 0 and s % tk == 0 and s >= WINDOW + tq and WINDOW % tq == 0 and tq & (tq - 1) == 0
    assert NSA_GROUP % chains == 0
    kv_spec = pl.BlockSpec((1, 1, s, LANES), lambda bi, gi, qi: (bi, gi, 0, 0))
    q_spec = pl.BlockSpec((1, tq, gw), lambda bi, gi, qi: (bi, qi, gi))
    rows = NSA_GROUP * tq
    return pl.pallas_call(
        functools.partial(_nsa_attn_kernel, tq=tq, tk=tk),
        out_shape=jax.ShapeDtypeStruct((b, s, NSA_HEADS * LANES), BF16),
        grid=(b, g, s // tq),
        in_specs=[q_spec, kv_spec, kv_spec, kv_spec, kv_spec,
                  pl.BlockSpec((1, 1, tq, LANES), lambda bi, gi, qi: (bi, gi, qi, 0)),
                  pl.BlockSpec((s, LANES), lambda bi, gi, qi: (0, 0)),
                  pl.BlockSpec((1, 1, gw, tq), lambda bi, gi, qi: (bi, gi, 0, qi)),
                  pl.BlockSpec((1, tq, LANES), lambda bi, gi, qi: (bi, qi, gate_col_block))],
        out_specs=q_spec,
        scratch_shapes=[pltpu.VMEM((s, 2 * LANES), BF16), pltpu.VMEM((rows, 2 * LANES), BF16),
                        pltpu.VMEM((chains, 1, rows // chains), F32), pltpu.VMEM((chains, 1, rows // chains), F32),
                        pltpu.VMEM((chains, LANES, rows // chains), F32)],
        compiler_params=_params("parallel", "parallel", "arbitrary"),
        name="nsa_slc_win",
    )(qr, ks, vs, kw, vw, bias, e_t, o_cmp, proj)


def _mla_attn_kernel(q_ref, kv_ref, kr_ref, tk_ref, o_ref, kcat_sc, m_sc, l_sc, acc_sc, *, tq, tk, heads):
    qi = pl.program_id(2)
    nope, dv = MLA_NOPE_DIM, MLA_V_DIM
    hq = kcat_sc.shape[2]

    @pl.when(qi == 0)
    def _():
        prod = kr_ref[0] * tk_ref[...]
        rope2 = prod + pltpu.roll(prod, MLA_ROPE_DIM, 1)
        for h in range(heads):
            kcat_sc[h, :, :nope] = kv_ref[0, :, h * (nope + dv):h * (nope + dv) + nope]
            kcat_sc[h, :, nope:] = rope2.astype(kcat_sc.dtype)

    q0 = qi * tq
    _flash_init(m_sc, l_sc, acc_sc)
    n_k = (q0 + tq + tk - 1) // tk

    def step(k0, mask):
        _flash_step([q_ref[0, :, h * hq:(h + 1) * hq] for h in range(heads)],
                    [kcat_sc[h, pl.ds(k0, tk), :] for h in range(heads)],
                    [kv_ref[0, pl.ds(k0, tk), h * (nope + dv) + nope:(h + 1) * (nope + dv)] for h in range(heads)],
                    m_sc, l_sc, acc_sc, mask=mask)

    def full(j, carry):
        step(pl.multiple_of(j * tk, tk), None)
        return carry

    lax.fori_loop(0, n_k - 1, full, 0)
    k0 = pl.multiple_of((n_k - 1) * tk, tk)
    causal = (k0 + lax.broadcasted_iota(jnp.int32, (tk, tq), 0)
              <= q0 + lax.broadcasted_iota(jnp.int32, (tk, tq), 1))
    step(k0, causal)
    for h in range(heads):
        o_ref[0, :, h * dv:(h + 1) * dv] = (acc_sc[h] / l_sc[h]).T.astype(o_ref.dtype)


def _mla_attn(q2, kv_up, proj, kr_col_block, tab_k, *, tq, tk, heads):
    b, s, _ = q2.shape
    hq = MLA_NOPE_DIM + 2 * MLA_ROPE_DIM
    hkv = MLA_NOPE_DIM + MLA_V_DIM
    assert tk % tq == 0 and s % tk == 0 and MLA_HEADS % heads == 0
    return pl.pallas_call(
        functools.partial(_mla_attn_kernel, tq=tq, tk=tk, heads=heads),
        out_shape=jax.ShapeDtypeStruct((b, s, MLA_HEADS * MLA_V_DIM), BF16),
        grid=(b, MLA_HEADS // heads, s // tq),
        in_specs=[pl.BlockSpec((1, tq, heads * hq), lambda bi, hi, qi: (bi, qi, hi)),
                  pl.BlockSpec((1, s, heads * hkv), lambda bi, hi, qi: (bi, 0, hi)),
                  pl.BlockSpec((1, s, LANES), lambda bi, hi, qi: (bi, 0, kr_col_block)),
                  pl.BlockSpec((s, LANES), lambda bi, hi, qi: (0, 0))],
        out_specs=pl.BlockSpec((1, tq, heads * MLA_V_DIM), lambda bi, hi, qi: (bi, qi, hi)),
        scratch_shapes=[pltpu.VMEM((heads, s, hq), BF16), pltpu.VMEM((heads, 1, tq), F32),
                        pltpu.VMEM((heads, 1, tq), F32), pltpu.VMEM((heads, MLA_V_DIM, tq), F32)],
        compiler_params=_params("parallel", "parallel", "arbitrary"),
        name="mla_attn",
    )(q2, kv_up, proj, tab_k)


def _ffn_kernel(x_ref, g_ref, w1_ref, w3_ref, w2_ref, o_ref, hn_sc, acc_sc):
    f = pl.program_id(1)

    @pl.when(f == 0)
    def _():
        hn_sc[...] = _rms(x_ref[...], g_ref[...]).astype(hn_sc.dtype)
        acc_sc[...] = jnp.zeros(acc_sc.shape, F32)

    hn = hn_sc[...]
    z = jax.nn.silu(_dot(hn, w1_ref[...])) * _dot(hn, w3_ref[...])
    acc_sc[...] += _dot(z, w2_ref[...])

    @pl.when(f == pl.num_programs(1) - 1)
    def _():
        o_ref[...] = x_ref[...] + acc_sc[...]


def _ffn(x, gain, w1, w3, w2, *, tm, tf):
    t, d = x.shape
    ff = w1.shape[1]
    return pl.pallas_call(
        _ffn_kernel,
        out_shape=jax.ShapeDtypeStruct((t, d), F32),
        grid=(t // tm, ff // tf),
        in_specs=[pl.BlockSpec((tm, d), lambda i, f: (i, 0)),
                  pl.BlockSpec((1, d), lambda i, f: (0, 0)),
                  pl.BlockSpec((d, tf), lambda i, f: (0, f)),
                  pl.BlockSpec((d, tf), lambda i, f: (0, f)),
                  pl.BlockSpec((tf, d), lambda i, f: (f, 0))],
        out_specs=pl.BlockSpec((tm, d), lambda i, f: (i, 0)),
        scratch_shapes=[pltpu.VMEM((tm, d), BF16), pltpu.VMEM((tm, d), F32)],
        compiler_params=_params("parallel", "arbitrary"),
        name="dense_ffn",
    )(x, gain.reshape(1, d), w1, w3, w2)


def _row_copy(src_ref, dst_ref, sem, src_row, dst_row):
    return pltpu.make_async_copy(src_ref.at[pl.ds(src_row, 1)], dst_ref.at[pl.ds(dst_row, 1)], sem)


def _gather_rows(idx_ref, src_ref, dst_ref, sem, n_rows, spread=False):
    def issue(r, carry):
        if spread:
            for k in range(2):
                _row_copy(src_ref, dst_ref, sem, idx_ref[2 * r + k], 2 * r + k).start(priority=k)
        else:
            _row_copy(src_ref, dst_ref, sem, idx_ref[r], r).start()
        return carry

    def drain(r, carry):
        _row_copy(src_ref, dst_ref, sem, idx_ref[r], r).wait()
        return carry

    lax.fori_loop(0, n_rows // 2 if spread else n_rows, issue, 0, unroll=8)
    lax.fori_loop(0, n_rows, drain, 0, unroll=8)


def _row_gather_kernel(idx_ref, src_ref, out_ref, sem, *, chunk):
    _gather_rows(idx_ref, src_ref, out_ref, sem, chunk, spread=True)


def _row_gather(src, idx, *, chunk, name):
    m = idx.shape[0]
    d = src.shape[1]
    assert chunk % 2 == 0
    return pl.pallas_call(
        functools.partial(_row_gather_kernel, chunk=chunk),
        out_shape=jax.ShapeDtypeStruct((m, d), src.dtype),
        grid=(m // chunk,),
        in_specs=[pl.BlockSpec((chunk,), lambda i: (i,), memory_space=pltpu.SMEM),
                  pl.BlockSpec(memory_space=pl.ANY)],
        out_specs=pl.BlockSpec((chunk, d), lambda i: (i, 0)),
        scratch_shapes=[pltpu.SemaphoreType.DMA(())],
        compiler_params=_params("arbitrary"),
        name=name,
    )(idx, src)


def _moe_ffn_kernel(te_ref, tl_ref, x_ref, g_ref, w1_ref, w3_ref, w2_ref, y_ref, hn_sc):
    i = pl.program_id(0)
    f = pl.program_id(1)
    live = tl_ref[i] > 0

    @pl.when(f == 0)
    def _():
        y_ref[...] = jnp.zeros(y_ref.shape, F32)

    @pl.when(live)
    def _():
        @pl.when(f == 0)
        def _():
            hn_sc[...] = _rms(x_ref[...], g_ref[...]).astype(hn_sc.dtype)

        hn = hn_sc[...]
        z = jax.nn.silu(_dot(hn, w1_ref[0])) * _dot(hn, w3_ref[0])
        y_ref[...] += _dot(z, w2_ref[0])


def _moe_ffn(x_sorted, gain, w1, w3, w2, tile_expert, tile_live, *, tm, tf):
    p, d = x_sorted.shape
    ff = w1.shape[2]

    def w_col(i, f, te, tl):
        return (te[i], 0, jnp.where(tl[i] > 0, f, 0))

    def w_row(i, f, te, tl):
        return (te[i], jnp.where(tl[i] > 0, f, 0), 0)

    return pl.pallas_call(
        _moe_ffn_kernel,
        out_shape=jax.ShapeDtypeStruct((p, d), F32),
        grid_spec=pltpu.PrefetchScalarGridSpec(
            num_scalar_prefetch=2,
            grid=(p // tm, ff // tf),
            in_specs=[pl.BlockSpec((tm, d), lambda i, f, te, tl: (i, 0)),
                      pl.BlockSpec((1, d), lambda i, f, te, tl: (0, 0)),
                      pl.BlockSpec((1, d, tf), w_col),
                      pl.BlockSpec((1, d, tf), w_col),
                      pl.BlockSpec((1, tf, d), w_row)],
            out_specs=pl.BlockSpec((tm, d), lambda i, f, te, tl: (i, 0)),
            scratch_shapes=[pltpu.VMEM((tm, d), BF16)]),
        compiler_params=_params("parallel", "arbitrary"),
        name="moe_ffn",
    )(tile_expert, tile_live, x_sorted, gain.reshape(1, d), w1, w3, w2)


def _moe_combine_kernel(*refs, final, tm):
    if final:
        slot_ref, x_ref, y_ref, r_ref, g_ref, o_ref, ybuf, sem = refs
    else:
        slot_ref, x_ref, y_ref, r_ref, o_ref, ybuf, sem = refs
    _gather_rows(slot_ref, y_ref, ybuf, sem, TOP_K * tm)
    route = r_ref[...]
    mix = None
    for k in range(TOP_K):
        term = route[:, TOP_K + k:TOP_K + k + 1] * ybuf[k * tm:(k + 1) * tm, :]
        mix = term if mix is None else mix + term
    out = x_ref[...] + mix
    if final:
        out = _rms(out, g_ref[...])
    o_ref[...] = out


def _moe_combine(x, y_sorted, slot_tiles, route, final_gain, *, tm):
    t, d = x.shape
    in_specs = [pl.BlockSpec((TOP_K * tm,), lambda i: (i,), memory_space=pltpu.SMEM),
                pl.BlockSpec((tm, d), lambda i: (i, 0)),
                pl.BlockSpec(memory_space=pl.ANY),
                pl.BlockSpec((tm, LANES), lambda i: (i, 0))]
    args = [slot_tiles, x, y_sorted, route]
    if final_gain is not None:
        in_specs.append(pl.BlockSpec((1, d), lambda i: (0, 0)))
        args.append(final_gain.reshape(1, d))
    return pl.pallas_call(
        functools.partial(_moe_combine_kernel, final=final_gain is not None, tm=tm),
        out_shape=jax.ShapeDtypeStruct((t, d), F32),
        grid=(t // tm,),
        in_specs=in_specs,
        out_specs=pl.BlockSpec((tm, d), lambda i: (i, 0)),
        scratch_shapes=[pltpu.VMEM((TOP_K * tm, d), F32), pltpu.SemaphoreType.DMA(())],
        compiler_params=_params("arbitrary"),
        name="moe_combine",
    )(*args)


def _router_kernel(x_ref, g_ref, w_ref, o_ref, *, n_exp):
    hn = _rms(x_ref[...], g_ref[...])
    w = w_ref[...]
    h_hi = hn.astype(BF16)
    h_lo = hn - h_hi.astype(F32)
    w_hi = w.astype(BF16)
    w_lo = w - w_hi.astype(F32)
    logits = _dot(h_hi, w_hi) + _dot(h_hi, w_lo) + _dot(h_lo, w_hi)
    lane = lax.broadcasted_iota(jnp.int32, logits.shape, 1)
    valid = lane < n_exp
    lg = jnp.where(valid, logits, NEG)
    ex = jnp.where(valid, jnp.exp(lg - jnp.max(lg, axis=-1, keepdims=True)), 0.0)
    probs = ex / jnp.sum(ex, axis=-1, keepdims=True)
    rest = jnp.where(valid, probs, -1.0)
    tops = []
    for _ in range(TOP_K):
        v = jnp.max(rest, axis=-1, keepdims=True)
        idx = jnp.min(jnp.where(rest == v, lane, LANES), axis=-1, keepdims=True)
        tops.append((v, idx))
        rest = jnp.where(lane == idx, -1.0, rest)
    den = tops[0][0]
    for v, _ in tops[1:]:
        den = den + v
    route = jnp.zeros(logits.shape, F32)
    for k, (v, idx) in enumerate(tops):
        route = jnp.where(lane == k, idx.astype(F32), route)
        route = jnp.where(lane == TOP_K + k, v / den, route)
    o_ref[...] = route


def _router(x, gain, router, *, tm):
    t, d = x.shape
    n_exp = router.shape[1]
    w = jnp.zeros((d, LANES), F32).at[:, :n_exp].set(router)
    return pl.pallas_call(
        functools.partial(_router_kernel, n_exp=n_exp),
        out_shape=jax.ShapeDtypeStruct((t, LANES), F32),
        grid=(t // tm,),
        in_specs=[pl.BlockSpec((tm, d), lambda i: (i, 0)),
                  pl.BlockSpec((1, d), lambda i: (0, 0)),
                  pl.BlockSpec((d, LANES), lambda i: (0, 0))],
        out_specs=pl.BlockSpec((tm, LANES), lambda i: (i, 0)),
        compiler_params=_params("parallel"),
        name="moe_router",
    )(x, gain.reshape(1, d), w)


def _final_norm_kernel(x_ref, g_ref, o_ref):
    o_ref[...] = _rms(x_ref[...], g_ref[...])


def _final_norm(x, gain, *, tm):
    t, d = x.shape
    return pl.pallas_call(
        _final_norm_kernel,
        out_shape=jax.ShapeDtypeStruct((t, d), F32),
        grid=(t // tm,),
        in_specs=[pl.BlockSpec((tm, d), lambda i: (i, 0)), pl.BlockSpec((1, d), lambda i: (0, 0))],
        out_specs=pl.BlockSpec((tm, d), lambda i: (i, 0)),
        compiler_params=_params("parallel"),
        name="final_norm",
    )(x, gain.reshape(1, d))


def _rope_tables(seq, dim):
    inv = 1.0 / (ROPE_THETA ** (jnp.arange(0, dim, 2, dtype=F32) / dim))
    ang = jnp.arange(seq, dtype=F32)[:, None] * inv[None, :]
    return jnp.cos(ang), jnp.sin(ang)


def _rot_half_cols(w):
    half = w.shape[-1] // 2
    return jnp.concatenate([w[..., half:], w[..., :half]], axis=-1)


def _tile(n, pref):
    for t in pref:
        if n % t == 0:
            return t
    return n


def kernel(x, attn_norm, w_in, w_out, cmp_k_pos, cmp_k_w1, cmp_k_w2, cmp_v_pos, cmp_v_w1, cmp_v_w2, mla_q_norm, mla_w_uq, mla_kv_norm, mla_w_ukv, ffn_norm, dense_w1, dense_w3, dense_w2, router, moe_w1, moe_w3, moe_w2, final_norm):
    b, s, d = x.shape
    depth = w_in.shape[0]
    t = b * s
    dh = LANES
    q_rank = mla_w_uq.shape[1]
    kv_rank = mla_w_ukv.shape[1]
    nsa_q = NSA_HEADS * dh
    nsa_kv = 6 * NSA_KV_HEADS * dh
    n_gate = 3 * NSA_HEADS
    o_q, o_kv, o_gate = 0, nsa_q, nsa_q + nsa_kv
    o_cq = o_gate + n_gate
    o_ckv = o_cq + q_rank
    o_kr = o_ckv + kv_rank
    assert o_kr + MLA_ROPE_DIM == w_in.shape[2]
    assert s % SLC_LEN == 0 and s % CMP_STRIDE == 0 and s // SLC_LEN <= LANES and s // CMP_STRIDE <= LANES
    c_cq = nsa_q + nsa_kv
    c_ckv = c_cq + q_rank
    c_kr = c_ckv + kv_rank
    c_gate = c_kr + 2 * MLA_ROPE_DIM
    n_in = c_gate + LANES
    assert q_rank % LANES == 0 and kv_rank % LANES == 0 and c_cq % q_rank == 0 and c_ckv % kv_rank == 0

    n_cmp = (s - CMP_LEN) // CMP_STRIDE + 1
    n_slc = s // SLC_LEN
    top_n = min(SLC_TOP, n_slc)

    cos_n, sin_n = _rope_tables(s, dh)
    cos_t = jnp.concatenate([cos_n, cos_n], axis=-1)
    sin_t = jnp.concatenate([-sin_n, sin_n], axis=-1)
    cos_m, sin_m = _rope_tables(s, MLA_ROPE_DIM)
    rot_cos = jnp.concatenate([cos_m, cos_m], axis=-1)
    rot_sin = jnp.concatenate([-sin_m, sin_m], axis=-1)
    mla_scale = (MLA_NOPE_DIM + MLA_ROPE_DIM) ** -0.5
    tab_q = jnp.concatenate([jnp.ones((s, MLA_NOPE_DIM), F32), rot_cos, rot_sin], axis=-1) * mla_scale
    tab_q = jnp.tile(tab_q, (1, MLA_HEADS))
    tab_k = jnp.concatenate([rot_cos, rot_sin], axis=-1)

    cmp_tok = jnp.arange(LANES)[:, None] * CMP_STRIDE + jnp.arange(CMP_LEN)[None, :]
    overlap = jnp.mean(((cmp_tok[:, :, None] // SLC_LEN) == jnp.arange(LANES)[None, None, :]).astype(F32), axis=1)
    overlap = jnp.where(jnp.arange(LANES)[None, :] < n_slc, overlap, 0.0).astype(BF16)
    overlap = overlap[:, :(n_slc + 7) // 8 * 8]
    e_t = ((jnp.arange(s)[:, None] // SLC_LEN) == jnp.arange(LANES)[None, :]).astype(BF16)

    tm = _tile(t, (512, 256, 128))
    tm_big = _tile(s, (1024, 512, 256, 128))
    tq = _tile(s, (256, 128))
    tk_att = _tile(s, (512, 256, 128))
    hq = MLA_NOPE_DIM + 2 * MLA_ROPE_DIM

    xf = x.reshape(t, d)
    for i in range(depth):
        wi = w_in[i]
        kr_w = wi[:, o_kr:o_kr + MLA_ROPE_DIM]
        w_in2 = jnp.concatenate(
            [wi[:, o_q:o_q + nsa_q], wi[:, o_kv:o_kv + nsa_kv], wi[:, o_cq:o_cq + q_rank],
             wi[:, o_ckv:o_ckv + kv_rank], kr_w, _rot_half_cols(kr_w), wi[:, o_gate:o_gate + n_gate],
             jnp.zeros((d, LANES - n_gate), F32)], axis=1).astype(BF16)
        proj = _mm([xf], [w_in2], gain=attn_norm[i], tm=tm_big, tn=_tile(n_in, (896, 512, 256, 128)),
                   out_dtype=F32, name="in_proj")
        proj3 = proj.reshape(b, s, n_in)

        qr, kc, vc, ks, vs, kw, vw = _nsa_prep(proj3, cos_t, sin_t, ts=_tile(s, (512, 256, 128)))
        k_cmp = _compress(kc, cmp_k_pos[i], cmp_k_w1[i], cmp_k_w2[i])
        v_cmp = _compress(vc, cmp_v_pos[i], cmp_v_w1[i], cmp_v_w2[i])
        o_cmp, sel = _cmp_sel(qr, k_cmp, v_cmp, overlap, tq=tq, n_cmp=n_cmp, n_slc=n_slc, top_n=top_n)
        o_nsa = _nsa_attn(qr, ks, vs, kw, vw, sel, e_t, o_cmp, proj3, c_gate // LANES, tq=tq, tk=tk_att, chains=4)

        wq = mla_w_uq[i].reshape(q_rank, MLA_HEADS, MLA_NOPE_DIM + MLA_ROPE_DIM)
        wq_rope = wq[..., MLA_NOPE_DIM:]
        wq2 = jnp.concatenate([wq, _rot_half_cols(wq_rope)], axis=-1).reshape(q_rank, MLA_HEADS * hq).astype(BF16)
        q2 = _mm([proj], [wq2], x_cols=[c_cq // q_rank], gain=mla_q_norm[i], tab=tab_q, tm=tm_big,
                 tn=MLA_HEADS * hq, out_dtype=BF16, name="mla_q_up")
        kv_up = _mm([proj], [mla_w_ukv[i].astype(BF16)], x_cols=[c_ckv // kv_rank], gain=mla_kv_norm[i],
                    tm=tm_big, tn=mla_w_ukv.shape[2], out_dtype=BF16, name="mla_kv_up")
        o_mla = _mla_attn(q2.reshape(b, s, -1), kv_up.reshape(b, s, -1), proj3, c_kr // LANES, tab_k, tq=tk_att,
                          tk=tk_att, heads=4)

        wo = w_out[i].astype(BF16)
        xf = _mm([o_nsa.reshape(t, -1), o_mla.reshape(t, -1)], [wo[:nsa_q], wo[nsa_q:]], res=xf, tm=tm_big,
                 tn=_tile(d, (1024, 512, 256, 128)), out_dtype=F32, name="out_proj")

        j = i // 2
        if i % 2 == 0:
            xf = _ffn(xf, ffn_norm[i], dense_w1[j].astype(BF16), dense_w3[j].astype(BF16),
                      dense_w2[j].astype(BF16), tm=tm, tf=_tile(dense_w1.shape[2], (512, 256, 128)))
        else:
            route = _router(xf, ffn_norm[i], router[j], tm=tm)
            n_exp = router.shape[2]
            pair_expert = route[:, :TOP_K].astype(jnp.int32).reshape(-1)
            onehot = (pair_expert[:, None] == jnp.arange(n_exp)[None, :]).astype(jnp.int32)
            csum = jnp.cumsum(onehot, axis=0)
            rank = jnp.sum(onehot * (csum - 1), axis=1)
            group_end = jnp.cumsum((csum[-1] + tm - 1) // tm * tm)
            group_start = group_end - (csum[-1] + tm - 1) // tm * tm
            slot = (group_start[pair_expert] + rank).astype(jnp.int32)
            n_tiles = TOP_K * t // tm + n_exp
            tile_start = jnp.arange(n_tiles) * tm
            tile_expert = jnp.minimum(jnp.sum(tile_start[:, None] >= group_end[None, :], axis=1),
                                      n_exp - 1).astype(jnp.int32)
            tile_live = (tile_start < group_end[-1]).astype(jnp.int32)
            pair_token = jnp.arange(TOP_K * t, dtype=jnp.int32) // TOP_K
            slot_token = jnp.zeros((n_tiles * tm,), jnp.int32).at[slot].set(pair_token)
            x_sorted = _row_gather(xf, slot_token, chunk=tm, name="moe_gather")
            y_sorted = _moe_ffn(x_sorted, ffn_norm[i], moe_w1[j].astype(BF16), moe_w3[j].astype(BF16),
                                moe_w2[j].astype(BF16), tile_expert, tile_live, tm=tm,
                                tf=_tile(moe_w1.shape[3], (1024, 512, 256, 128)))
            slot_tiles = slot.reshape(t // tm, tm, TOP_K).transpose(0, 2, 1).reshape(-1)
            last = i == depth - 1
            xf = _moe_combine(xf, y_sorted, slot_tiles, route, final_norm if last else None, tm=tm)
            if last:
                return xf.reshape(b, s, d)
    return _final_norm(xf, final_norm, tm=tm).reshape(b, s, d)
```

```python
import functools

import jax
import jax.numpy as jnp
from jax import lax
from jax.experimental import pallas as pl
from jax.experimental.pallas import tpu as pltpu

F32 = jnp.float32
BF16 = jnp.bfloat16

NSA_HEADS = 8
NSA_KV_HEADS = 2
NSA_GROUP = NSA_HEADS // NSA_KV_HEADS
CMP_LEN = 32
CMP_STRIDE = 16
SLC_LEN = 64
SLC_SHIFT = 6
SLC_TOP = 16
N_FORCED_LOCAL = 2
FORCE_SCORE = 1.0e4
WINDOW = 512
MLA_HEADS = 8
MLA_NOPE_DIM = 128
MLA_ROPE_DIM = 64
MLA_V_DIM = 128
ROPE_THETA = 10000.0
RMS_EPS = 1e-6
TOP_K = 2

LANES = 128
NEG = -1.0e30
VMEM_LIMIT = 56 * 1024 * 1024


def _params(*sem):
    return pltpu.CompilerParams(dimension_semantics=sem, vmem_limit_bytes=VMEM_LIMIT)


def _rms(xf, g):
    return xf * lax.rsqrt(jnp.mean(xf * xf, axis=-1, keepdims=True) + RMS_EPS) * g


def _dot(a, b):
    return jnp.dot(a.astype(BF16), b.astype(BF16), preferred_element_type=F32)


def _dot_nt(a, b):
    return lax.dot_general(a.astype(BF16), b.astype(BF16), (((1,), (1,)), ((), ())),
                           preferred_element_type=F32)


def _mm_kernel(*refs, n_x, has_norm, has_tab, has_res):
    x_refs = refs[:n_x]
    w_refs = refs[n_x:2 * n_x]
    pos = 2 * n_x
    g_ref = tab_ref = res_ref = None
    if has_norm:
        g_ref = refs[pos]
        pos += 1
    if has_tab:
        tab_ref = refs[pos]
        pos += 1
    if has_res:
        res_ref = refs[pos]
        pos += 1
    o_ref = refs[pos]
    if has_norm:
        hn_sc = refs[pos + 1]

        @pl.when(pl.program_id(1) == 0)
        def _():
            hn_sc[...] = _rms(x_refs[0][...].astype(F32), g_ref[...]).astype(hn_sc.dtype)

    acc = None
    for x_ref, w_ref in zip(x_refs, w_refs):
        x = hn_sc[...] if has_norm else x_ref[...]
        d = _dot(x, w_ref[...])
        acc = d if acc is None else acc + d
    if has_tab:
        acc = acc * tab_ref[...]
    if has_res:
        acc = acc + res_ref[...]
    o_ref[...] = acc.astype(o_ref.dtype)


def _mm(xs, ws, *, x_cols=None, gain=None, tab=None, res=None, tm, tn, out_dtype, name):
    n_x = len(xs)
    assert gain is None or n_x == 1
    m = xs[0].shape[0]
    n = ws[0].shape[1]
    x_cols = x_cols or [0] * n_x
    in_specs, args = [], []
    for x, w, c in zip(xs, ws, x_cols):
        in_specs.append(pl.BlockSpec((tm, w.shape[0]), lambda i, j, c=c: (i, c)))
        args.append(x)
    for w in ws:
        in_specs.append(pl.BlockSpec((w.shape[0], tn), lambda i, j: (0, j)))
        args.append(w)
    if gain is not None:
        in_specs.append(pl.BlockSpec((1, gain.shape[-1]), lambda i, j: (0, 0)))
        args.append(gain.reshape(1, -1))
    if tab is not None:
        rb, cb = tab.shape[0] // tm, tab.shape[1] // tn
        in_specs.append(pl.BlockSpec((tm, tn), lambda i, j: (i % rb, j % cb)))
        args.append(tab)
    if res is not None:
        in_specs.append(pl.BlockSpec((tm, tn), lambda i, j: (i, j)))
        args.append(res)
    return pl.pallas_call(
        functools.partial(_mm_kernel, n_x=n_x, has_norm=gain is not None, has_tab=tab is not None,
                          has_res=res is not None),
        out_shape=jax.ShapeDtypeStruct((m, n), out_dtype),
        grid=(m // tm, n // tn),
        in_specs=in_specs,
        out_specs=pl.BlockSpec((tm, tn), lambda i, j: (i, j)),
        scratch_shapes=[pltpu.VMEM((tm, ws[0].shape[0]), BF16)] if gain is not None else [],
        compiler_params=_params("parallel", "arbitrary"),
        name=name,
    )(*args)


def _nsa_prep_kernel(p_ref, cos_ref, sin_ref, q_ref, kc_ref, vc_ref, ks_ref, vs_ref, kw_ref, vw_ref):
    cos = cos_ref[...]
    sin = sin_ref[...]
    dh = LANES

    def rope(x):
        return x * cos + pltpu.roll(x, dh // 2, 1) * sin

    scale = dh ** -0.5
    for h in range(NSA_HEADS):
        q_ref[0, :, h * dh:(h + 1) * dh] = (rope(p_ref[0, :, h * dh:(h + 1) * dh]) * scale).astype(q_ref.dtype)
    base = NSA_HEADS * dh
    outs = (kc_ref, vc_ref, ks_ref, vs_ref, kw_ref, vw_ref)
    for t, o_ref in enumerate(outs):
        for g in range(NSA_KV_HEADS):
            c0 = base + (t * NSA_KV_HEADS + g) * dh
            v = p_ref[0, :, c0:c0 + dh]
            if t % 2 == 0:
                v = rope(v)
            o_ref[0, g] = v.astype(o_ref.dtype)


def _nsa_prep(proj, cos_t, sin_t, *, ts):
    b, s, _ = proj.shape
    g, dh = NSA_KV_HEADS, LANES
    width = (NSA_HEADS + 6 * g) * dh
    kv_spec = pl.BlockSpec((1, g, ts, dh), lambda bi, si: (bi, 0, si, 0))
    kv = lambda dt: jax.ShapeDtypeStruct((b, g, s, dh), dt)
    return pl.pallas_call(
        _nsa_prep_kernel,
        out_shape=(jax.ShapeDtypeStruct((b, s, NSA_HEADS * dh), BF16),
                   kv(F32), kv(F32), kv(BF16), kv(BF16), kv(BF16), kv(BF16)),
        grid=(b, s // ts),
        in_specs=[pl.BlockSpec((1, ts, width), lambda bi, si: (bi, si, 0)),
                  pl.BlockSpec((ts, dh), lambda bi, si: (si, 0)),
                  pl.BlockSpec((ts, dh), lambda bi, si: (si, 0))],
        out_specs=(pl.BlockSpec((1, ts, NSA_HEADS * dh), lambda bi, si: (bi, si, 0)),
                   kv_spec, kv_spec, kv_spec, kv_spec, kv_spec, kv_spec),
        compiler_params=_params("parallel", "parallel"),
        name="nsa_prep",
    )(proj, cos_t, sin_t)


def _compress_kernel(t_ref, pa_ref, pb_ref, w1a_ref, w1b_ref, w2_ref, o_ref):
    t = t_ref[0, 0]
    ya = _dot(t + pa_ref[...], w1a_ref[...])
    yb = _dot(t + pb_ref[...], w1b_ref[...])
    n = yb.shape[0]
    pre = ya + pltpu.roll(yb, n - 1, 0)
    o_ref[0, 0] = _dot(jax.nn.gelu(pre), w2_ref[...])


def _compress(t, pos_emb, w1, w2):
    b, g, s, dh = t.shape
    nb = s // CMP_STRIDE
    half = CMP_STRIDE * dh
    hid = w1.shape[1]
    t16 = t.reshape(b, g, nb, half)
    pa = pos_emb[:CMP_STRIDE].reshape(1, half)
    pb = pos_emb[CMP_STRIDE:].reshape(1, half)
    const = lambda shape: pl.BlockSpec(shape, lambda bi, gi: (0, 0))
    return pl.pallas_call(
        _compress_kernel,
        out_shape=jax.ShapeDtypeStruct((b, g, nb, dh), F32),
        grid=(b, g),
        in_specs=[pl.BlockSpec((1, 1, nb, half), lambda bi, gi: (bi, gi, 0, 0)),
                  const((1, half)), const((1, half)),
                  const((half, hid)), const((half, hid)), const((hid, dh))],
        out_specs=pl.BlockSpec((1, 1, nb, dh), lambda bi, gi: (bi, gi, 0, 0)),
        compiler_params=_params("parallel", "parallel"),
        name="nsa_compress",
    )(t16, pa, pb, w1[:half].astype(BF16), w1[half:].astype(BF16), w2.astype(BF16))


def _cmp_sel_kernel(q_ref, kc_ref, vc_ref, ov_ref, ocmp_ref, sel_ref, *, tq, n_cmp, n_slc, top_n):
    qi = pl.program_id(2)
    dh = LANES
    n_c = kc_ref.shape[2]
    n_j = ov_ref.shape[1]
    pos = qi * tq + lax.broadcasted_iota(jnp.int32, (n_c, tq), 1)
    c = lax.broadcasted_iota(jnp.int32, (n_c, tq), 0)
    valid = (c * CMP_STRIDE + (CMP_LEN - 1) <= pos) & (c < n_cmp)
    kc = kc_ref[0, 0]
    vc = vc_ref[0, 0]
    psum = jnp.zeros((n_c, tq), F32)
    for r in range(NSA_GROUP):
        s = jnp.where(valid, _dot_nt(kc, q_ref[0, :, r * dh:(r + 1) * dh]), NEG)
        m = jnp.max(s, axis=0, keepdims=True)
        p = jnp.where(valid, jnp.exp(s - m), 0.0)
        p = p / jnp.maximum(jnp.sum(p, axis=0, keepdims=True), 1e-30)
        ocmp_ref[0, 0, r * dh:(r + 1) * dh, :] = _dot_tn(vc, p)
        psum = psum + p
    hi = psum.astype(BF16)
    lo = psum - hi.astype(F32)
    imp = _dot_tn(ov_ref[...], hi) + _dot_tn(ov_ref[...], lo)
    pos = qi * tq + lax.broadcasted_iota(jnp.int32, (n_j, tq), 1)
    j = lax.broadcasted_iota(jnp.int32, (n_j, tq), 0)
    causal_blk = j * SLC_LEN <= pos
    dist = lax.shift_right_logical(pos, SLC_SHIFT) - j
    forced = (j == 0) | ((dist >= 0) & (dist < N_FORCED_LOCAL))
    score = jnp.where(causal_blk, jnp.where(forced, FORCE_SCORE, imp), -jnp.inf)
    rank = jnp.zeros((n_j, tq), jnp.int32)
    for i in range(n_slc):
        row = score[i:i + 1, :]
        beats = (row > score) | ((row == score) & (j > i))
        rank = rank + jnp.where(beats, 1, 0)
    bias = jnp.where((rank < top_n) | (j >= n_slc), 0.0, NEG)
    if n_j < LANES:
        bias = jnp.concatenate([bias, jnp.zeros((LANES - n_j, tq), F32)], axis=0)
    sel_ref[0, 0] = bias.T.astype(sel_ref.dtype)


def _cmp_sel(qr, k_cmp, v_cmp, overlap, *, tq, n_cmp, n_slc, top_n):
    b, s, _ = qr.shape
    g = NSA_KV_HEADS
    gw = NSA_GROUP * LANES
    nb = k_cmp.shape[2]
    return pl.pallas_call(
        functools.partial(_cmp_sel_kernel, tq=tq, n_cmp=n_cmp, n_slc=n_slc, top_n=top_n),
        out_shape=(jax.ShapeDtypeStruct((b, g, gw, s), F32),
                   jax.ShapeDtypeStruct((b, g, s, LANES), BF16)),
        grid=(b, g, s // tq),
        in_specs=[pl.BlockSpec((1, tq, gw), lambda bi, gi, qi: (bi, qi, gi)),
                  pl.BlockSpec((1, 1, nb, LANES), lambda bi, gi, qi: (bi, gi, 0, 0)),
                  pl.BlockSpec((1, 1, nb, LANES), lambda bi, gi, qi: (bi, gi, 0, 0)),
                  pl.BlockSpec(overlap.shape, lambda bi, gi, qi: (0, 0))],
        out_specs=(pl.BlockSpec((1, 1, gw, tq), lambda bi, gi, qi: (bi, gi, 0, qi)),
                   pl.BlockSpec((1, 1, tq, LANES), lambda bi, gi, qi: (bi, gi, qi, 0))),
        compiler_params=_params("parallel", "parallel", "parallel"),
        name="nsa_cmp_select",
    )(qr, k_cmp, v_cmp, overlap)


def _flash_init(m_sc, l_sc, acc_sc):
    m_sc[...] = jnp.full(m_sc.shape, NEG, F32)
    l_sc[...] = jnp.zeros(l_sc.shape, F32)
    acc_sc[...] = jnp.zeros(acc_sc.shape, F32)


def _dot_tn(a, b):
    return lax.dot_general(a.astype(BF16), b.astype(BF16), (((0,), (0,)), ((), ())),
                           preferred_element_type=F32)


def _flash_step(qs, ks, vs, m_sc, l_sc, acc_sc, mask=None):
    chains = range(len(qs))
    scores = [_dot_nt(ks[c], qs[c]) for c in chains]
    probs = []
    for c in chains:
        s = scores[c] if mask is None else jnp.where(mask, scores[c], NEG)
        m_prev = m_sc[c]
        m_new = jnp.maximum(m_prev, jnp.max(s, axis=0, keepdims=True))
        alpha = jnp.exp(m_prev - m_new)
        p = jnp.exp(s - m_new)
        l_sc[c] = alpha * l_sc[c] + jnp.sum(p, axis=0, keepdims=True)
        m_sc[c] = m_new
        probs.append((alpha, p.astype(BF16)))
    for c in chains:
        alpha, p = probs[c]
        acc_sc[c] = alpha * acc_sc[c] + _dot_tn(vs[c], p)


def _nsa_attn_kernel(q_ref, ks_ref, vs_ref, kw_ref, vw_ref, bias_ref, et_ref, ocmp_ref, gate_ref, o_ref,
                     kaug_sc, qaug_sc, m_sc, l_sc, acc_sc, *, tq, tk):
    gi = pl.program_id(1)
    qi = pl.program_id(2)
    dh = LANES
    s_len = kaug_sc.shape[0]
    q0 = qi * tq

    @pl.when(qi == 0)
    def _():
        kaug_sc[:, :dh] = ks_ref[0, 0]
        kaug_sc[:, dh:] = et_ref[...]

    bias = bias_ref[0, 0]
    for r in range(NSA_GROUP):
        qaug_sc[r * tq:(r + 1) * tq, :dh] = q_ref[0, :, r * dh:(r + 1) * dh]
        qaug_sc[r * tq:(r + 1) * tq, dh:] = bias

    chains = m_sc.shape[0]
    n_q = NSA_GROUP * tq // chains

    def q_chain(c, width):
        return qaug_sc[c * n_q:(c + 1) * n_q, :width]

    def attend(k, v, width, mask):
        _flash_step([q_chain(c, width) for c in range(chains)], [k] * chains, [v] * chains,
                    m_sc, l_sc, acc_sc, mask=mask)

    def result():
        return [acc_sc[c] / l_sc[c] for c in range(chains)]

    def q_pos(n_keys):
        return q0 + (lax.broadcasted_iota(jnp.int32, (n_keys, n_q), 1) & (tq - 1))

    _flash_init(m_sc, l_sc, acc_sc)
    n_k = (q0 + tq + tk - 1) // tk

    def slc_full(j, carry):
        k0 = pl.multiple_of(j * tk, tk)
        attend(kaug_sc[pl.ds(k0, tk), :], vs_ref[0, 0, pl.ds(k0, tk), :], 2 * dh, None)
        return carry

    lax.fori_loop(0, n_k - 1, slc_full, 0)
    k0 = pl.multiple_of((n_k - 1) * tk, tk)
    causal = k0 + lax.broadcasted_iota(jnp.int32, (tk, n_q), 0) <= q_pos(tk)
    attend(kaug_sc[pl.ds(k0, tk), :], vs_ref[0, 0, pl.ds(k0, tk), :], 2 * dh, causal)
    o_slc = result()

    band = WINDOW + tq
    start = pl.multiple_of(jnp.clip(q0 - WINDOW, 0, s_len - band), tq)
    diff = q_pos(band) - (start + lax.broadcasted_iota(jnp.int32, (band, n_q), 0))
    _flash_init(m_sc, l_sc, acc_sc)
    attend(kw_ref[0, 0, pl.ds(start, band), :], vw_ref[0, 0, pl.ds(start, band), :], dh,
           (diff >= 0) & (diff < WINDOW))
    o_win = result()

    gates_t = jax.nn.sigmoid(gate_ref[0]).T
    n_gate = 3 * NSA_GROUP

    def gate(r, branch):
        out = None
        for g in range(NSA_KV_HEADS):
            c = g * n_gate + r * 3 + branch
            v = gates_t[c:c + 1, :]
            out = v if out is None else jnp.where(gi == g, v, out)
        return out

    per_chain = NSA_GROUP // chains
    for r in range(NSA_GROUP):
        c = r // per_chain
        cols = slice((r % per_chain) * tq, (r % per_chain + 1) * tq)
        mixed_t = (gate(r, 0) * ocmp_ref[0, 0, r * dh:(r + 1) * dh, :] + gate(r, 1) * o_slc[c][:, cols]
                   + gate(r, 2) * o_win[c][:, cols])
        o_ref[0, :, r * dh:(r + 1) * dh] = mixed_t.T.astype(o_ref.dtype)


def _nsa_attn(qr, ks, vs, kw, vw, bias, e_t, o_cmp, proj, gate_col_block, *, tq, tk, chains):
    b, s, _ = qr.shape
    g = NSA_KV_HEADS
    gw = NSA_GROUP * LANES
    assert tk % tq == 0 and s % tk == 0 and s >= WINDOW + tq and WINDOW % tq == 0 and tq & (tq - 1) == 0
    assert NSA_GROUP % chains == 0
    kv_spec = pl.BlockSpec((1, 1, s, LANES), lambda bi, gi, qi: (bi, gi, 0, 0))
    q_spec = pl.BlockSpec((1, tq, gw), lambda bi, gi, qi: (bi, qi, gi))
    rows = NSA_GROUP * tq
    return pl.pallas_call(
        functools.partial(_nsa_attn_kernel, tq=tq, tk=tk),
        out_shape=jax.ShapeDtypeStruct((b, s, NSA_HEADS * LANES), BF16),
        grid=(b, g, s // tq),
        in_specs=[q_spec, kv_spec, kv_spec, kv_spec, kv_spec,
                  pl.BlockSpec((1, 1, tq, LANES), lambda bi, gi, qi: (bi, gi, qi, 0)),
                  pl.BlockSpec((s, LANES), lambda bi, gi, qi: (0, 0)),
                  pl.BlockSpec((1, 1, gw, tq), lambda bi, gi, qi: (bi, gi, 0, qi)),
                  pl.BlockSpec((1, tq, LANES), lambda bi, gi, qi: (bi, qi, gate_col_block))],
        out_specs=q_spec,
        scratch_shapes=[pltpu.VMEM((s, 2 * LANES), BF16), pltpu.VMEM((rows, 2 * LANES), BF16),
                        pltpu.VMEM((chains, 1, rows // chains), F32), pltpu.VMEM((chains, 1, rows // chains), F32),
                        pltpu.VMEM((chains, LANES, rows // chains), F32)],
        compiler_params=_params("parallel", "parallel", "arbitrary"),
        name="nsa_slc_win",
    )(qr, ks, vs, kw, vw, bias, e_t, o_cmp, proj)


def _mla_attn_kernel(q_ref, kv_ref, kr_ref, tk_ref, o_ref, kcat_sc, m_sc, l_sc, acc_sc, *, tq, tk, heads):
    qi = pl.program_id(2)
    nope, dv = MLA_NOPE_DIM, MLA_V_DIM
    hq = kcat_sc.shape[2]

    @pl.when(qi == 0)
    def _():
        prod = kr_ref[0] * tk_ref[...]
        rope2 = prod + pltpu.roll(prod, MLA_ROPE_DIM, 1)
        for h in range(heads):
            kcat_sc[h, :, :nope] = kv_ref[0, :, h * (nope + dv):h * (nope + dv) + nope]
            kcat_sc[h, :, nope:] = rope2.astype(kcat_sc.dtype)

    q0 = qi * tq
    _flash_init(m_sc, l_sc, acc_sc)
    n_k = (q0 + tq + tk - 1) // tk

    def step(k0, mask):
        _flash_step([q_ref[0, :, h * hq:(h + 1) * hq] for h in range(heads)],
                    [kcat_sc[h, pl.ds(k0, tk), :] for h in range(heads)],
                    [kv_ref[0, pl.ds(k0, tk), h * (nope + dv) + nope:(h + 1) * (nope + dv)] for h in range(heads)],
                    m_sc, l_sc, acc_sc, mask=mask)

    def full(j, carry):
        step(pl.multiple_of(j * tk, tk), None)
        return carry

    lax.fori_loop(0, n_k - 1, full, 0)
    k0 = pl.multiple_of((n_k - 1) * tk, tk)
    causal = (k0 + lax.broadcasted_iota(jnp.int32, (tk, tq), 0)
              <= q0 + lax.broadcasted_iota(jnp.int32, (tk, tq), 1))
    step(k0, causal)
    for h in range(heads):
        o_ref[0, :, h * dv:(h + 1) * dv] = (acc_sc[h] / l_sc[h]).T.astype(o_ref.dtype)


def _mla_attn(q2, kv_up, proj, kr_col_block, tab_k, *, tq, tk, heads):
    b, s, _ = q2.shape
    hq = MLA_NOPE_DIM + 2 * MLA_ROPE_DIM
    hkv = MLA_NOPE_DIM + MLA_V_DIM
    assert tk % tq == 0 and s % tk == 0 and MLA_HEADS % heads == 0
    return pl.pallas_call(
        functools.partial(_mla_attn_kernel, tq=tq, tk=tk, heads=heads),
        out_shape=jax.ShapeDtypeStruct((b, s, MLA_HEADS * MLA_V_DIM), BF16),
        grid=(b, MLA_HEADS // heads, s // tq),
        in_specs=[pl.BlockSpec((1, tq, heads * hq), lambda bi, hi, qi: (bi, qi, hi)),
                  pl.BlockSpec((1, s, heads * hkv), lambda bi, hi, qi: (bi, 0, hi)),
                  pl.BlockSpec((1, s, LANES), lambda bi, hi, qi: (bi, 0, kr_col_block)),
                  pl.BlockSpec((s, LANES), lambda bi, hi, qi: (0, 0))],
        out_specs=pl.BlockSpec((1, tq, heads * MLA_V_DIM), lambda bi, hi, qi: (bi, qi, hi)),
        scratch_shapes=[pltpu.VMEM((heads, s, hq), BF16), pltpu.VMEM((heads, 1, tq), F32),
                        pltpu.VMEM((heads, 1, tq), F32), pltpu.VMEM((heads, MLA_V_DIM, tq), F32)],
        compiler_params=_params("parallel", "parallel", "arbitrary"),
        name="mla_attn",
    )(q2, kv_up, proj, tab_k)


def _ffn_kernel(x_ref, g_ref, w1_ref, w3_ref, w2_ref, o_ref, hn_sc, acc_sc):
    f = pl.program_id(1)

    @pl.when(f == 0)
    def _():
        hn_sc[...] = _rms(x_ref[...], g_ref[...]).astype(hn_sc.dtype)
        acc_sc[...] = jnp.zeros(acc_sc.shape, F32)

    hn = hn_sc[...]
    z = jax.nn.silu(_dot(hn, w1_ref[...])) * _dot(hn, w3_ref[...])
    acc_sc[...] += _dot(z, w2_ref[...])

    @pl.when(f == pl.num_programs(1) - 1)
    def _():
        o_ref[...] = x_ref[...] + acc_sc[...]


def _ffn(x, gain, w1, w3, w2, *, tm, tf):
    t, d = x.shape
    ff = w1.shape[1]
    return pl.pallas_call(
        _ffn_kernel,
        out_shape=jax.ShapeDtypeStruct((t, d), F32),
        grid=(t // tm, ff // tf),
        in_specs=[pl.BlockSpec((tm, d), lambda i, f: (i, 0), pipeline_mode=pl.Buffered(1)),
                  pl.BlockSpec((1, d), lambda i, f: (0, 0)),
                  pl.BlockSpec((d, tf), lambda i, f: (0, f)),
                  pl.BlockSpec((d, tf), lambda i, f: (0, f)),
                  pl.BlockSpec((tf, d), lambda i, f: (f, 0))],
        out_specs=pl.BlockSpec((tm, d), lambda i, f: (i, 0), pipeline_mode=pl.Buffered(1)),
        scratch_shapes=[pltpu.VMEM((tm, d), BF16), pltpu.VMEM((tm, d), F32)],
        compiler_params=_params("parallel", "arbitrary"),
        name="dense_ffn",
    )(x, gain.reshape(1, d), w1, w3, w2)


def _row_copy(src_ref, dst_ref, sem, src_row, dst_row):
    return pltpu.make_async_copy(src_ref.at[pl.ds(src_row, 1)], dst_ref.at[pl.ds(dst_row, 1)], sem)


def _gather_rows(idx_ref, src_ref, dst_ref, sem, n_rows, spread=False):
    def issue(r, carry):
        if spread:
            for k in range(2):
                _row_copy(src_ref, dst_ref, sem, idx_ref[2 * r + k], 2 * r + k).start(priority=k)
        else:
            _row_copy(src_ref, dst_ref, sem, idx_ref[r], r).start()
        return carry

    def drain(r, carry):
        _row_copy(src_ref, dst_ref, sem, idx_ref[r], r).wait()
        return carry

    lax.fori_loop(0, n_rows // 2 if spread else n_rows, issue, 0, unroll=8)
    lax.fori_loop(0, n_rows, drain, 0, unroll=8)


def _row_gather_kernel(idx_ref, src_ref, out_ref, sem, *, chunk):
    _gather_rows(idx_ref, src_ref, out_ref, sem, chunk, spread=True)


def _row_gather(src, idx, *, chunk, name):
    m = idx.shape[0]
    d = src.shape[1]
    assert chunk % 2 == 0
    return pl.pallas_call(
        functools.partial(_row_gather_kernel, chunk=chunk),
        out_shape=jax.ShapeDtypeStruct((m, d), src.dtype),
        grid=(m // chunk,),
        in_specs=[pl.BlockSpec((chunk,), lambda i: (i,), memory_space=pltpu.SMEM),
                  pl.BlockSpec(memory_space=pl.ANY)],
        out_specs=pl.BlockSpec((chunk, d), lambda i: (i, 0)),
        scratch_shapes=[pltpu.SemaphoreType.DMA(())],
        compiler_params=_params("arbitrary"),
        name=name,
    )(idx, src)


def _moe_ffn_kernel(te_ref, tl_ref, x_ref, g_ref, w1_ref, w3_ref, w2_ref, y_ref, hn_sc):
    i = pl.program_id(0)
    f = pl.program_id(1)
    live = tl_ref[i] > 0

    @pl.when(f == 0)
    def _():
        y_ref[...] = jnp.zeros(y_ref.shape, F32)

    @pl.when(live)
    def _():
        @pl.when(f == 0)
        def _():
            hn_sc[...] = _rms(x_ref[...], g_ref[...]).astype(hn_sc.dtype)

        hn = hn_sc[...]
        z = jax.nn.silu(_dot(hn, w1_ref[0])) * _dot(hn, w3_ref[0])
        y_ref[...] += _dot(z, w2_ref[0])


def _moe_ffn(x_sorted, gain, w1, w3, w2, tile_expert, tile_live, *, tm, tf):
    p, d = x_sorted.shape
    ff = w1.shape[2]

    def w_col(i, f, te, tl):
        return (te[i], 0, jnp.where(tl[i] > 0, f, 0))

    def w_row(i, f, te, tl):
        return (te[i], jnp.where(tl[i] > 0, f, 0), 0)

    return pl.pallas_call(
        _moe_ffn_kernel,
        out_shape=jax.ShapeDtypeStruct((p, d), F32),
        grid_spec=pltpu.PrefetchScalarGridSpec(
            num_scalar_prefetch=2,
            grid=(p // tm, ff // tf),
            in_specs=[pl.BlockSpec((tm, d), lambda i, f, te, tl: (i, 0)),
                      pl.BlockSpec((1, d), lambda i, f, te, tl: (0, 0)),
                      pl.BlockSpec((1, d, tf), w_col),
                      pl.BlockSpec((1, d, tf), w_col),
                      pl.BlockSpec((1, tf, d), w_row)],
            out_specs=pl.BlockSpec((tm, d), lambda i, f, te, tl: (i, 0)),
            scratch_shapes=[pltpu.VMEM((tm, d), BF16)]),
        compiler_params=_params("parallel", "arbitrary"),
        name="moe_ffn",
    )(tile_expert, tile_live, x_sorted, gain.reshape(1, d), w1, w3, w2)


def _moe_combine_kernel(*refs, final, tm):
    if final:
        slot_ref, x_ref, y_ref, r_ref, g_ref, o_ref, ybuf, sem = refs
    else:
        slot_ref, x_ref, y_ref, r_ref, o_ref, ybuf, sem = refs
    _gather_rows(slot_ref, y_ref, ybuf, sem, TOP_K * tm)
    route = r_ref[...]
    mix = None
    for k in range(TOP_K):
        term = route[:, TOP_K + k:TOP_K + k + 1] * ybuf[k * tm:(k + 1) * tm, :]
        mix = term if mix is None else mix + term
    out = x_ref[...] + mix
    if final:
        out = _rms(out, g_ref[...])
    o_ref[...] = out


def _moe_combine(x, y_sorted, slot_tiles, route, final_gain, *, tm):
    t, d = x.shape
    in_specs = [pl.BlockSpec((TOP_K * tm,), lambda i: (i,), memory_space=pltpu.SMEM),
                pl.BlockSpec((tm, d), lambda i: (i, 0)),
                pl.BlockSpec(memory_space=pl.ANY),
                pl.BlockSpec((tm, LANES), lambda i: (i, 0))]
    args = [slot_tiles, x, y_sorted, route]
    if final_gain is not None:
        in_specs.append(pl.BlockSpec((1, d), lambda i: (0, 0)))
        args.append(final_gain.reshape(1, d))
    return pl.pallas_call(
        functools.partial(_moe_combine_kernel, final=final_gain is not None, tm=tm),
        out_shape=jax.ShapeDtypeStruct((t, d), F32),
        grid=(t // tm,),
        in_specs=in_specs,
        out_specs=pl.BlockSpec((tm, d), lambda i: (i, 0)),
        scratch_shapes=[pltpu.VMEM((TOP_K * tm, d), F32), pltpu.SemaphoreType.DMA(())],
        compiler_params=_params("arbitrary"),
        name="moe_combine",
    )(*args)


def _router_kernel(x_ref, g_ref, w_ref, o_ref, *, n_exp):
    hn = _rms(x_ref[...], g_ref[...])
    w = w_ref[...]
    h_hi = hn.astype(BF16)
    h_lo = hn - h_hi.astype(F32)
    w_hi = w.astype(BF16)
    w_lo = w - w_hi.astype(F32)
    logits = _dot(h_hi, w_hi) + _dot(h_hi, w_lo) + _dot(h_lo, w_hi)
    lane = lax.broadcasted_iota(jnp.int32, logits.shape, 1)
    valid = lane < n_exp
    lg = jnp.where(valid, logits, NEG)
    ex = jnp.where(valid, jnp.exp(lg - jnp.max(lg, axis=-1, keepdims=True)), 0.0)
    probs = ex / jnp.sum(ex, axis=-1, keepdims=True)
    rest = jnp.where(valid, probs, -1.0)
    tops = []
    for _ in range(TOP_K):
        v = jnp.max(rest, axis=-1, keepdims=True)
        idx = jnp.min(jnp.where(rest == v, lane, LANES), axis=-1, keepdims=True)
        tops.append((v, idx))
        rest = jnp.where(lane == idx, -1.0, rest)
    den = tops[0][0]
    for v, _ in tops[1:]:
        den = den + v
    route = jnp.zeros(logits.shape, F32)
    for k, (v, idx) in enumerate(tops):
        route = jnp.where(lane == k, idx.astype(F32), route)
        route = jnp.where(lane == TOP_K + k, v / den, route)
    o_ref[...] = route


def _router(x, gain, router, *, tm):
    t, d = x.shape
    n_exp = router.shape[1]
    w = jnp.zeros((d, LANES), F32).at[:, :n_exp].set(router)
    return pl.pallas_call(
        functools.partial(_router_kernel, n_exp=n_exp),
        out_shape=jax.ShapeDtypeStruct((t, LANES), F32),
        grid=(t // tm,),
        in_specs=[pl.BlockSpec((tm, d), lambda i: (i, 0)),
                  pl.BlockSpec((1, d), lambda i: (0, 0)),
                  pl.BlockSpec((d, LANES), lambda i: (0, 0))],
        out_specs=pl.BlockSpec((tm, LANES), lambda i: (i, 0)),
        compiler_params=_params("parallel"),
        name="moe_router",
    )(x, gain.reshape(1, d), w)


def _final_norm_kernel(x_ref, g_ref, o_ref):
    o_ref[...] = _rms(x_ref[...], g_ref[...])


def _final_norm(x, gain, *, tm):
    t, d = x.shape
    return pl.pallas_call(
        _final_norm_kernel,
        out_shape=jax.ShapeDtypeStruct((t, d), F32),
        grid=(t // tm,),
        in_specs=[pl.BlockSpec((tm, d), lambda i: (i, 0)), pl.BlockSpec((1, d), lambda i: (0, 0))],
        out_specs=pl.BlockSpec((tm, d), lambda i: (i, 0)),
        compiler_params=_params("parallel"),
        name="final_norm",
    )(x, gain.reshape(1, d))


def _rope_tables(seq, dim):
    inv = 1.0 / (ROPE_THETA ** (jnp.arange(0, dim, 2, dtype=F32) / dim))
    ang = jnp.arange(seq, dtype=F32)[:, None] * inv[None, :]
    return jnp.cos(ang), jnp.sin(ang)


def _rot_half_cols(w):
    half = w.shape[-1] // 2
    return jnp.concatenate([w[..., half:], w[..., :half]], axis=-1)


def _tile(n, pref):
    for t in pref:
        if n % t == 0:
            return t
    return n


def kernel(x, attn_norm, w_in, w_out, cmp_k_pos, cmp_k_w1, cmp_k_w2, cmp_v_pos, cmp_v_w1, cmp_v_w2, mla_q_norm, mla_w_uq, mla_kv_norm, mla_w_ukv, ffn_norm, dense_w1, dense_w3, dense_w2, router, moe_w1, moe_w3, moe_w2, final_norm):
    b, s, d = x.shape
    depth = w_in.shape[0]
    t = b * s
    dh = LANES
    q_rank = mla_w_uq.shape[1]
    kv_rank = mla_w_ukv.shape[1]
    nsa_q = NSA_HEADS * dh
    nsa_kv = 6 * NSA_KV_HEADS * dh
    n_gate = 3 * NSA_HEADS
    o_q, o_kv, o_gate = 0, nsa_q, nsa_q + nsa_kv
    o_cq = o_gate + n_gate
    o_ckv = o_cq + q_rank
    o_kr = o_ckv + kv_rank
    assert o_kr + MLA_ROPE_DIM == w_in.shape[2]
    assert s % SLC_LEN == 0 and s % CMP_STRIDE == 0 and s // SLC_LEN <= LANES and s // CMP_STRIDE <= LANES
    c_cq = nsa_q + nsa_kv
    c_ckv = c_cq + q_rank
    c_kr = c_ckv + kv_rank
    c_gate = c_kr + 2 * MLA_ROPE_DIM
    n_in = c_gate + LANES
    assert q_rank % LANES == 0 and kv_rank % LANES == 0 and c_cq % q_rank == 0 and c_ckv % kv_rank == 0

    n_cmp = (s - CMP_LEN) // CMP_STRIDE + 1
    n_slc = s // SLC_LEN
    top_n = min(SLC_TOP, n_slc)

    cos_n, sin_n = _rope_tables(s, dh)
    cos_t = jnp.concatenate([cos_n, cos_n], axis=-1)
    sin_t = jnp.concatenate([-sin_n, sin_n], axis=-1)
    cos_m, sin_m = _rope_tables(s, MLA_ROPE_DIM)
    rot_cos = jnp.concatenate([cos_m, cos_m], axis=-1)
    rot_sin = jnp.concatenate([-sin_m, sin_m], axis=-1)
    mla_scale = (MLA_NOPE_DIM + MLA_ROPE_DIM) ** -0.5
    tab_q = jnp.concatenate([jnp.ones((s, MLA_NOPE_DIM), F32), rot_cos, rot_sin], axis=-1) * mla_scale
    tab_q = jnp.tile(tab_q, (1, MLA_HEADS))
    tab_k = jnp.concatenate([rot_cos, rot_sin], axis=-1)

    cmp_tok = jnp.arange(LANES)[:, None] * CMP_STRIDE + jnp.arange(CMP_LEN)[None, :]
    overlap = jnp.mean(((cmp_tok[:, :, None] // SLC_LEN) == jnp.arange(LANES)[None, None, :]).astype(F32), axis=1)
    overlap = jnp.where(jnp.arange(LANES)[None, :] < n_slc, overlap, 0.0).astype(BF16)
    overlap = overlap[:, :(n_slc + 7) // 8 * 8]
    e_t = ((jnp.arange(s)[:, None] // SLC_LEN) == jnp.arange(LANES)[None, :]).astype(BF16)

    tm = _tile(t, (512, 256, 128))
    tm_big = _tile(s, (1024, 512, 256, 128))
    tq = _tile(s, (256, 128))
    tk_att = _tile(s, (512, 256, 128))
    hq = MLA_NOPE_DIM + 2 * MLA_ROPE_DIM

    xf = x.reshape(t, d)
    for i in range(depth):
        wi = w_in[i]
        kr_w = wi[:, o_kr:o_kr + MLA_ROPE_DIM]
        w_in2 = jnp.concatenate(
            [wi[:, o_q:o_q + nsa_q], wi[:, o_kv:o_kv + nsa_kv], wi[:, o_cq:o_cq + q_rank],
             wi[:, o_ckv:o_ckv + kv_rank], kr_w, _rot_half_cols(kr_w), wi[:, o_gate:o_gate + n_gate],
             jnp.zeros((d, LANES - n_gate), F32)], axis=1).astype(BF16)
        proj = _mm([xf], [w_in2], gain=attn_norm[i], tm=tm_big, tn=_tile(n_in, (896, 512, 256, 128)),
                   out_dtype=F32, name="in_proj")
        proj3 = proj.reshape(b, s, n_in)

        qr, kc, vc, ks, vs, kw, vw = _nsa_prep(proj3, cos_t, sin_t, ts=_tile(s, (512, 256, 128)))
        k_cmp = _compress(kc, cmp_k_pos[i], cmp_k_w1[i], cmp_k_w2[i])
        v_cmp = _compress(vc, cmp_v_pos[i], cmp_v_w1[i], cmp_v_w2[i])
        o_cmp, sel = _cmp_sel(qr, k_cmp, v_cmp, overlap, tq=tq, n_cmp=n_cmp, n_slc=n_slc, top_n=top_n)
        o_nsa = _nsa_attn(qr, ks, vs, kw, vw, sel, e_t, o_cmp, proj3, c_gate // LANES, tq=tq, tk=tk_att, chains=4)

        wq = mla_w_uq[i].reshape(q_rank, MLA_HEADS, MLA_NOPE_DIM + MLA_ROPE_DIM)
        wq_rope = wq[..., MLA_NOPE_DIM:]
        wq2 = jnp.concatenate([wq, _rot_half_cols(wq_rope)], axis=-1).reshape(q_rank, MLA_HEADS * hq).astype(BF16)
        q2 = _mm([proj], [wq2], x_cols=[c_cq // q_rank], gain=mla_q_norm[i], tab=tab_q, tm=tm_big,
                 tn=MLA_HEADS * hq, out_dtype=BF16, name="mla_q_up")
        kv_up = _mm([proj], [mla_w_ukv[i].astype(BF16)], x_cols=[c_ckv // kv_rank], gain=mla_kv_norm[i],
                    tm=tm_big, tn=mla_w_ukv.shape[2], out_dtype=BF16, name="mla_kv_up")
        o_mla = _mla_attn(q2.reshape(b, s, -1), kv_up.reshape(b, s, -1), proj3, c_kr // LANES, tab_k, tq=tk_att,
                          tk=tk_att, heads=4)

        wo = w_out[i].astype(BF16)
        xf = _mm([o_nsa.reshape(t, -1), o_mla.reshape(t, -1)], [wo[:nsa_q], wo[nsa_q:]], res=xf, tm=tm_big,
                 tn=_tile(d, (1024, 512, 256, 128)), out_dtype=F32, name="out_proj")

        j = i // 2
        if i % 2 == 0:
            xf = _ffn(xf, ffn_norm[i], dense_w1[j].astype(BF16), dense_w3[j].astype(BF16),
                      dense_w2[j].astype(BF16), tm=tm_big, tf=_tile(dense_w1.shape[2], (512, 256, 128)))
        else:
            route = _router(xf, ffn_norm[i], router[j], tm=tm)
            n_exp = router.shape[2]
            pair_expert = route[:, :TOP_K].astype(jnp.int32).reshape(-1)
            onehot = (pair_expert[:, None] == jnp.arange(n_exp)[None, :]).astype(jnp.int32)
            csum = jnp.cumsum(onehot, axis=0)
            rank = jnp.sum(onehot * (csum - 1), axis=1)
            group_end = jnp.cumsum((csum[-1] + tm - 1) // tm * tm)
            group_start = group_end - (csum[-1] + tm - 1) // tm * tm
            slot = (group_start[pair_expert] + rank).astype(jnp.int32)
            n_tiles = TOP_K * t // tm + n_exp
            tile_start = jnp.arange(n_tiles) * tm
            tile_expert = jnp.minimum(jnp.sum(tile_start[:, None] >= group_end[None, :], axis=1),
                                      n_exp - 1).astype(jnp.int32)
            tile_live = (tile_start < group_end[-1]).astype(jnp.int32)
            pair_token = jnp.arange(TOP_K * t, dtype=jnp.int32) // TOP_K
            slot_token = jnp.zeros((n_tiles * tm,), jnp.int32).at[slot].set(pair_token)
            x_sorted = _row_gather(xf, slot_token, chunk=tm, name="moe_gather")
            y_sorted = _moe_ffn(x_sorted, ffn_norm[i], moe_w1[j].astype(BF16), moe_w3[j].astype(BF16),
                                moe_w2[j].astype(BF16), tile_expert, tile_live, tm=tm,
                                tf=_tile(moe_w1.shape[3], (1024, 512, 256, 128)))
            slot_tiles = slot.reshape(t // tm, tm, TOP_K).transpose(0, 2, 1).reshape(-1)
            last = i == depth - 1
            xf = _moe_combine(xf, y_sorted, slot_tiles, route, final_norm if last else None, tm=tm)
            if last:
                return xf.reshape(b, s, d)
    return _final_norm(xf, final_norm, tm=tm).reshape(b, s, d)
```
